```python
import math
import jax, jax.numpy as jnp
from jax import lax
import numpy as np

D_MODEL = 2048
BATCH = 4
SEQ = 4096
DEPTH = 2

GRID_W = 64
Q_BLOCK = 128
NORM_EPS = 1e-6
ROPE_THETA = 10000.0
NEG_INF = -1e30

A_HEADS = 8
A_DQK = 64
A_DV = 2 * A_DQK
A_WIDTH = A_HEADS * A_DV
B_HEADS = 8
B_NOPE = 128
B_ROPE = 64
B_DV = 128
B_Q_RANK = 768
B_KV_RANK = 512
B_WIDTH = B_HEADS * B_DV
C_HEADS = 8
C_DH = 128
C_WIDTH = C_HEADS * C_DH
NA_KH = 8
NA_KW = 16
D_HEADS = 8
D_KV_HEADS = 2
D_DH = 128
D_WIDTH = D_HEADS * D_DH

EVEN_SPLITS = (A_HEADS * 2 * A_DQK, A_HEADS * 2 * A_DQK, A_WIDTH, A_WIDTH,
               B_Q_RANK, B_KV_RANK, B_ROPE, B_WIDTH)
ODD_SPLITS = (C_WIDTH, C_WIDTH, C_WIDTH, C_WIDTH,
              D_HEADS * D_DH, D_KV_HEADS * D_DH, D_KV_HEADS * D_DH, D_WIDTH)
EVEN_IN = sum(EVEN_SPLITS)
ODD_IN = sum(ODD_SPLITS)
N_EVEN = (DEPTH + 1) // 2
N_ODD = DEPTH // 2

kernel_name = "hybrid_diffattn_mla_natten_axialgqa_encoder"


def rms_norm(x, gain):
    xf = x.astype(jnp.float32)
    y = xf * lax.rsqrt(jnp.mean(xf * xf, axis=-1, keepdims=True) + NORM_EPS)
    return (y * gain.astype(jnp.float32)).astype(x.dtype)


def split_cols(h, sizes):
    idx = np.cumsum(sizes)[:-1].tolist()
    return jnp.split(h, idx, axis=-1)


def rope(x, pos):
    half = x.shape[-1] // 2
    inv = ROPE_THETA ** (-jnp.arange(half, dtype=jnp.float32) / half)
    ang = pos.astype(jnp.float32)[:, None] * inv[None, :]
    shape = (1, x.shape[1]) + (1,) * (x.ndim - 3) + (half,)
    cos, sin = jnp.cos(ang).reshape(shape), jnp.sin(ang).reshape(shape)
    xf = x.astype(jnp.float32)
    x1, x2 = xf[..., :half], xf[..., half:]
    return jnp.concatenate([x1 * cos - x2 * sin, x1 * sin + x2 * cos], axis=-1).astype(x.dtype)


def axial_rope(x, row, col):
    half = x.shape[-1] // 2
    return jnp.concatenate([rope(x[..., :half], row), rope(x[..., half:], col)], axis=-1)


def to_blocks(t):
    b, s = t.shape[:2]
    return jnp.moveaxis(t.reshape((b, s // Q_BLOCK, Q_BLOCK) + t.shape[2:]), 1, 0)


def from_blocks(t):
    t = jnp.moveaxis(t, 0, 1)
    return t.reshape((t.shape[0], t.shape[1] * t.shape[2]) + t.shape[3:])


def diff_attention(q, k, v, lam, subln, lambda_init):
    b, s = q.shape[:2]
    q = q.reshape(b, s, A_HEADS, 2, A_DQK)
    k = k.reshape(b, s, A_HEADS, 2, A_DQK)
    v = v.reshape(b, s, A_HEADS, A_DV)
    pos = jnp.arange(s, dtype=jnp.float32)
    slopes = 2.0 ** (-8.0 * jnp.arange(1, A_HEADS + 1, dtype=jnp.float32) / A_HEADS)
    scale = A_DQK ** -0.5

    def block(args):
        qb, pb = args
        sc = jnp.einsum('bqhmd,bkhmd->bhmqk', qb, k).astype(jnp.float32) * scale
        alibi = -slopes[:, None, None] * jnp.abs(pb[:, None] - pos[None, :])[None]
        p = jax.nn.softmax(sc + alibi[None, :, None], axis=-1)
        w = p[:, :, 0] - lam * p[:, :, 1]
        return jnp.einsum('bhqk,bkhd->bqhd', w.astype(v.dtype), v)

    o = from_blocks(lax.map(block, (to_blocks(q), pos.reshape(-1, Q_BLOCK))))
    o = rms_norm(o, subln) * (1.0 - lambda_init)
    return o.reshape(b, s, A_WIDTH)


def latent_attention(cq, ckv, k_rope, q_norm, w_uq, kv_norm, w_ukv):
    b, s = cq.shape[:2]
    pos = jnp.arange(s)
    q = (rms_norm(cq, q_norm) @ w_uq).reshape(b, s, B_HEADS, B_NOPE + B_ROPE)
    q_nope, q_pe = q[..., :B_NOPE], rope(q[..., B_NOPE:], pos)
    kv = (rms_norm(ckv, kv_norm) @ w_ukv).reshape(b, s, B_HEADS, B_NOPE + B_DV)
    k_nope, v = kv[..., :B_NOPE], kv[..., B_NOPE:]
    k_pe = rope(k_rope, pos)
    scale = (B_NOPE + B_ROPE) ** -0.5

    def block(args):
        qn, qp = args
        sc = (jnp.einsum('bqhd,bkhd->bhqk', qn, k_nope)
              + jnp.einsum('bqhr,bkr->bhqk', qp, k_pe))
        p = jax.nn.softmax(sc.astype(jnp.float32) * scale, axis=-1)
        return jnp.einsum('bhqk,bkhd->bqhd', p.astype(v.dtype), v)

    o = from_blocks(lax.map(block, (to_blocks(q_nope), to_blocks(q_pe))))
    return o.reshape(b, s, B_WIDTH)


def neighbourhood_attention(q, k, v, rpb):
    b, s = q.shape[:2]
    rows = s // GRID_W
    kh = min(NA_KH, rows)
    scale = C_DH ** -0.5
    qg = q.reshape(b, rows, GRID_W, C_HEADS, C_DH)
    kg = k.reshape(b, rows, GRID_W, C_HEADS, C_DH)
    vg = v.reshape(b, rows, GRID_W, C_HEADS, C_DH)
    r = jnp.arange(rows)
    r0 = jnp.clip(r - kh // 2, 0, rows - kh)
    row_idx = r0[:, None] + jnp.arange(kh)[None, :]
    kb = kg[:, row_idx]
    vb = vg[:, row_idx]
    c = jnp.arange(GRID_W)
    c0 = jnp.clip(c - NA_KW // 2, 0, GRID_W - NA_KW)
    col_in = (c[None, :] >= c0[:, None]) & (c[None, :] < c0[:, None] + NA_KW)
    dr = row_idx - r[:, None] + NA_KH - 1
    dc = jnp.clip(c[None, :] - c[:, None], -(NA_KW - 1), NA_KW - 1) + NA_KW - 1
    bias = rpb[:, dr[:, None, :, None], dc[None, :, None, :]]
    sc = jnp.einsum('brqhd,brkwhd->bhrqkw', qg, kb).astype(jnp.float32) * scale + bias[None]
    sc = jnp.where(col_in[:, None, :], sc, NEG_INF)
    p = jax.nn.softmax(sc, axis=(-2, -1))
    o = jnp.einsum('bhrqkw,brkwhd->brqhd', p.astype(vb.dtype), vb)
    return o.reshape(b, s, C_WIDTH)


def axial_gqa(q, k, v, q_norm, k_norm):
    b, s = q.shape[:2]
    t = jnp.arange(s)
    row, col = t // GRID_W, t % GRID_W
    q = axial_rope(rms_norm(q.reshape(b, s, D_HEADS, D_DH), q_norm), row, col)
    k = axial_rope(rms_norm(k.reshape(b, s, D_KV_HEADS, D_DH), k_norm), row, col)
    q = q.reshape(b, s, D_KV_HEADS, D_HEADS // D_KV_HEADS, D_DH)
    v = v.reshape(b, s, D_KV_HEADS, D_DH)
    scale = D_DH ** -0.5

    def block(qb):
        sc = jnp.einsum('bqhgd,bkhd->bhgqk', qb, k).astype(jnp.float32) * scale
        p = jax.nn.softmax(sc, axis=-1)
        return jnp.einsum('bhgqk,bkhd->bqhgd', p.astype(v.dtype), v)

    o = from_blocks(lax.map(block, to_blocks(q)))
    return o.reshape(b, s, D_WIDTH)


def even_mixer(h, w_in, w_out, lq1, lk1, lq2, lk2, subln, q_norm, w_uq, kv_norm, w_ukv, lambda_init):
    qa, ka, va, ga, cq, ckv, kr, gb = split_cols(h @ w_in, EVEN_SPLITS)
    lam = (jnp.exp(jnp.sum(lq1.astype(jnp.float32) * lk1.astype(jnp.float32)))
           - jnp.exp(jnp.sum(lq2.astype(jnp.float32) * lk2.astype(jnp.float32))) + lambda_init)
    oa = diff_attention(qa, ka, va, lam, subln, lambda_init) * jax.nn.silu(ga)
    ob = latent_attention(cq, ckv, kr, q_norm, w_uq, kv_norm, w_ukv) * jax.nn.silu(gb)
    return jnp.concatenate([oa, ob], axis=-1) @ w_out


def odd_mixer(h, w_in, w_out, rpb, q_norm, k_norm):
    qc, kc, vc, gc, qd, kd, vd, gd = split_cols(h @ w_in, ODD_SPLITS)
    oc = neighbourhood_attention(qc, kc, vc, rpb) * jax.nn.silu(gc)
    od = axial_gqa(qd, kd, vd, q_norm, k_norm) * jax.nn.silu(gd)
    return jnp.concatenate([oc, od], axis=-1) @ w_out


def setup_inputs(seed: int = 0) -> dict:
    key = jax.random.key(seed)
    ks = jax.random.split(key, 20)

    def nrm(k, shape, scale):
        return jax.random.normal(k, shape, jnp.float32) * scale

    def gain(k, shape):
        return 1.0 + 0.02 * jax.random.normal(k, shape, jnp.float32)

    return {
        "x": nrm(ks[0], (BATCH, SEQ, D_MODEL), 1.0),
        "pre_norm": gain(ks[1], (DEPTH, D_MODEL)),
        "post_norm": gain(ks[2], (DEPTH, D_MODEL)),
        "even_w_in": nrm(ks[3], (N_EVEN, D_MODEL, EVEN_IN), D_MODEL ** -0.5),
        "even_w_out": nrm(ks[4], (N_EVEN, A_WIDTH + B_WIDTH, D_MODEL), (A_WIDTH + B_WIDTH) ** -0.5),
        "diff_lambda_q1": nrm(ks[5], (N_EVEN, A_DQK), 0.1),
        "diff_lambda_k1": nrm(ks[6], (N_EVEN, A_DQK), 0.1),
        "diff_lambda_q2": nrm(ks[7], (N_EVEN, A_DQK), 0.1),
        "diff_lambda_k2": nrm(ks[8], (N_EVEN, A_DQK), 0.1),
        "diff_subln": gain(ks[9], (N_EVEN, A_DV)),
        "mla_q_norm": gain(ks[10], (N_EVEN, B_Q_RANK)),
        "mla_w_uq": nrm(ks[11], (N_EVEN, B_Q_RANK, B_HEADS * (B_NOPE + B_ROPE)), B_Q_RANK ** -0.5),
        "mla_kv_norm": gain(ks[12], (N_EVEN, B_KV_RANK)),
        "mla_w_ukv": nrm(ks[13], (N_EVEN, B_KV_RANK, B_HEADS * (B_NOPE + B_DV)), B_KV_RANK ** -0.5),
        "odd_w_in": nrm(ks[14], (N_ODD, D_MODEL, ODD_IN), D_MODEL ** -0.5),
        "odd_w_out": nrm(ks[15], (N_ODD, C_WIDTH + D_WIDTH, D_MODEL), (C_WIDTH + D_WIDTH) ** -0.5),
        "na_rpb": nrm(ks[16], (N_ODD, C_HEADS, 2 * NA_KH - 1, 2 * NA_KW - 1), 0.1),
        "gqa_q_norm": gain(ks[17], (N_ODD, D_DH)),
        "gqa_k_norm": gain(ks[18], (N_ODD, D_DH)),
    }


def reference(x, pre_norm, post_norm, even_w_in, even_w_out, diff_lambda_q1, diff_lambda_k1,
              diff_lambda_q2, diff_lambda_k2, diff_subln, mla_q_norm, mla_w_uq, mla_kv_norm,
              mla_w_ukv, odd_w_in, odd_w_out, na_rpb, gqa_q_norm, gqa_k_norm):
    for layer in range(DEPTH):
        i = layer // 2
        h = rms_norm(x, pre_norm[layer])
        if layer % 2 == 0:
            lambda_init = 0.8 - 0.6 * math.exp(-0.3 * layer)
            m = even_mixer(h, even_w_in[i], even_w_out[i], diff_lambda_q1[i], diff_lambda_k1[i],
                           diff_lambda_q2[i], diff_lambda_k2[i], diff_subln[i], mla_q_norm[i],
                           mla_w_uq[i], mla_kv_norm[i], mla_w_ukv[i], lambda_init)
        else:
            m = odd_mixer(h, odd_w_in[i], odd_w_out[i], na_rpb[i], gqa_q_norm[i], gqa_k_norm[i])
        x = x + rms_norm(m, post_norm[layer])
    return x
```

```python
import functools
import math

import numpy as np
import jax
import jax.numpy as jnp
from jax import lax
from jax.experimental import pallas as pl
from jax.experimental.pallas import tpu as pltpu

F32 = jnp.float32
BF16 = jnp.bfloat16

D_MODEL = 2048
SEQ = 4096
GRID_W = 64
NORM_EPS = 1e-6
ROPE_THETA = 10000.0
NEG_INF = -1e30

HEADS = 8
HEAD_W = 128
A_DQK = 64
B_NOPE = 128
B_ROPE = 64
B_Q_RANK = 768
B_KV_RANK = 512
B_QK_PAD = 256
D_KV_HEADS = 2
NA_KH = 8
NA_KW = 16
NA_GROUP_ROWS = 4
NA_BAND_ROWS = 12

ROPE_SEG = 64
ROPE_HALF = 32

VMEM_LIMIT = 56 * 1024 * 1024

TM_PROJ = 256
TQ = 512
TK = 512


def _rms(xf, gain):
    ms = jnp.mean(xf * xf, axis=-1, keepdims=True)
    return xf * lax.rsqrt(ms + NORM_EPS) * gain


def _silu(g):
    return g * (1.0 / (1.0 + jnp.exp(-g)))


def _rope(x, cos, sin_signed):
    w = x.shape[-1]
    lane = lax.broadcasted_iota(jnp.int32, x.shape, 1)
    first_half = (lane & (ROPE_SEG - 1)) < ROPE_HALF
    partner = jnp.where(first_half, pltpu.roll(x, w - ROPE_HALF, 1), pltpu.roll(x, ROPE_HALF, 1))
    return x * cos + partner * sin_signed


def _dot(a, b):
    return jnp.dot(a, b, preferred_element_type=F32)


def _dot_nt(a, b):
    return lax.dot_general(a, b, (((1,), (1,)), ((), ())), preferred_element_type=F32)


def _flash(q, k_ref, v_ref, bias_fn=None):
    tq = q.shape[0]
    dv = v_ref.shape[-1]
    n_chunks = k_ref.shape[0] // TK

    def body(c, carry):
        m, l, acc = carry
        start = pl.multiple_of(c * TK, TK)
        s = _dot_nt(q, k_ref[pl.ds(start, TK), :])
        if bias_fn is not None:
            s = s + bias_fn(c)
        m_new = jnp.maximum(m, jnp.max(s, axis=1, keepdims=True))
        alpha = jnp.exp(m - m_new)
        p = jnp.exp(s - m_new)
        l = alpha * l + jnp.sum(p, axis=1, keepdims=True)
        acc = alpha * acc + _dot(p.astype(BF16), v_ref[pl.ds(start, TK), :])
        return m_new, l, acc

    init = (jnp.full((tq, 1), NEG_INF, F32), jnp.zeros((tq, 1), F32), jnp.zeros((tq, dv), F32))
    _, l, acc = lax.fori_loop(0, n_chunks, body, init)
    return acc, l


_E_QKV, _E_GATE, _E_CQ, _E_CKV, _E_KR = 0, 3072, 5120, 5888, 6400
_E_COLS = 6528
_N_CHUNK = 512


def _even_inproj_kernel(x_ref, pg_ref, w_ref, qn_ref, kvn_ref, cos_ref, sin_ref,
                        qkv_ref, gate_ref, cq_ref, ckv_ref, kpe_ref):
    h = _rms(x_ref[...], pg_ref[...]).astype(BF16)

    def proj(c0, n):
        return _dot(h, w_ref[:, c0:c0 + n])

    for c in range(0, 1024, _N_CHUNK):
        qkv_ref[:, c:c + _N_CHUNK] = (proj(_E_QKV + c, _N_CHUNK) * (A_DQK ** -0.5)).astype(BF16)
    for c in range(1024, 3072, _N_CHUNK):
        qkv_ref[:, c:c + _N_CHUNK] = proj(_E_QKV + c, _N_CHUNK).astype(BF16)
    for c in range(0, 2048, _N_CHUNK):
        gate_ref[:, c:c + _N_CHUNK] = _silu(proj(_E_GATE + c, _N_CHUNK))
    cq_ref[...] = _rms(proj(_E_CQ, B_Q_RANK), qn_ref[...]).astype(BF16)
    ckv_ref[...] = _rms(proj(_E_CKV, B_KV_RANK), kvn_ref[...]).astype(BF16)
    kpe_ref[...] = _rope(proj(_E_KR, 128), cos_ref[...], sin_ref[...]).astype(BF16)


def _even_inproj(x2, pre_gain, w_perm, q_norm, kv_norm, cos_k, sin_k):
    m = x2.shape[0]
    tm = TM_PROJ
    pos_blocks = SEQ // tm
    row = lambda i: (i, 0)
    fixed = lambda i: (0, 0)
    pos = lambda i: (i % pos_blocks, 0)
    return pl.pallas_call(
        _even_inproj_kernel,
        grid=(m // tm,),
        in_specs=[
            pl.BlockSpec((tm, D_MODEL), row),
            pl.BlockSpec((1, D_MODEL), fixed),
            pl.BlockSpec((D_MODEL, _E_COLS), fixed, pipeline_mode=pl.Buffered(1)),
            pl.BlockSpec((1, B_Q_RANK), fixed),
            pl.BlockSpec((1, B_KV_RANK), fixed),
            pl.BlockSpec((tm, 128), pos),
            pl.BlockSpec((tm, 128), pos),
        ],
        out_specs=[
            pl.BlockSpec((tm, 3072), row),
            pl.BlockSpec((tm, 2048), row),
            pl.BlockSpec((tm, B_Q_RANK), row),
            pl.BlockSpec((tm, B_KV_RANK), row),
            pl.BlockSpec((tm, 128), row),
        ],
        out_shape=[
            jax.ShapeDtypeStruct((m, 3072), BF16),
            jax.ShapeDtypeStruct((m, 2048), F32),
            jax.ShapeDtypeStruct((m, B_Q_RANK), BF16),
            jax.ShapeDtypeStruct((m, B_KV_RANK), BF16),
            jax.ShapeDtypeStruct((m, 128), BF16),
        ],
        compiler_params=pltpu.CompilerParams(
            dimension_semantics=("arbitrary",), vmem_limit_bytes=VMEM_LIMIT),
        name="even_inproj",
    )(x2, pre_gain, w_perm, q_norm, kv_norm, cos_k, sin_k)


def _diff_attn_kernel(slopes_ref, q_ref, k_ref, v_ref, g_ref, lq1_ref, lk1_ref, lq2_ref, lk2_ref,
                      subln_ref, o_ref, *, lambda_init):
    head = pl.program_id(1)
    q_start = pl.program_id(2) * TQ
    slope = slopes_ref[head]
    q = q_ref[...]
    lane = lax.broadcasted_iota(jnp.int32, q.shape, 1)
    zero = jnp.zeros_like(q)
    q_maps = (jnp.where(lane < A_DQK, q, zero), jnp.where(lane >= A_DQK, q, zero))

    rel = (lax.broadcasted_iota(jnp.int32, (TQ, TK), 0)
           - lax.broadcasted_iota(jnp.int32, (TQ, TK), 1)).astype(F32)

    def alibi(c):
        offset = (q_start - c * TK).astype(F32)
        return -slope * jnp.abs(rel + offset)

    outs = []
    for qm in q_maps:
        acc, l = _flash(qm, k_ref, v_ref, alibi)
        outs.append(acc / l)

    lam = (jnp.exp(jnp.sum(lq1_ref[...] * lk1_ref[...], axis=-1, keepdims=True))
           - jnp.exp(jnp.sum(lq2_ref[...] * lk2_ref[...], axis=-1, keepdims=True))
           + lambda_init)
    o = outs[0] - lam * outs[1]
    o = _rms(o, subln_ref[...]) * (1.0 - lambda_init)
    o_ref[...] = (o * g_ref[...]).astype(BF16)


def _diff_attn(slopes, qkv, gate, lq1, lk1, lq2, lk2, subln, lambda_init):
    b, s, _ = qkv.shape
    vec = lambda bb, h, qi: (0, 0)
    return pl.pallas_call(
        functools.partial(_diff_attn_kernel, lambda_init=lambda_init),
        grid=(b, HEADS, s // TQ),
        in_specs=[
            pl.BlockSpec(memory_space=pltpu.SMEM),
            pl.BlockSpec((None, TQ, HEAD_W), lambda bb, h, qi: (bb, qi, h)),
            pl.BlockSpec((None, s, HEAD_W), lambda bb, h, qi: (bb, 0, HEADS + h)),
            pl.BlockSpec((None, s, HEAD_W), lambda bb, h, qi: (bb, 0, 2 * HEADS + h)),
            pl.BlockSpec((None, TQ, HEAD_W), lambda bb, h, qi: (bb, qi, h)),
            pl.BlockSpec((1, A_DQK), vec),
            pl.BlockSpec((1, A_DQK), vec),
            pl.BlockSpec((1, A_DQK), vec),
            pl.BlockSpec((1, A_DQK), vec),
            pl.BlockSpec((1, HEAD_W), vec),
        ],
        out_specs=pl.BlockSpec((None, TQ, HEAD_W), lambda bb, h, qi: (bb, qi, h)),
        out_shape=jax.ShapeDtypeStruct((b, s, HEADS * HEAD_W), BF16),
        compiler_params=pltpu.CompilerParams(
            dimension_semantics=("arbitrary",) * 3, vmem_limit_bytes=VMEM_LIMIT),
        name="diff_attn",
    )(slopes, qkv, qkv, qkv, gate, lq1, lk1, lq2, lk2, subln)


def _mla_up_kernel(cq_ref, ckv_ref, kpe_ref, wq_ref, wk_ref, wv_ref, cos_ref, sin_ref,
                   q_ref, k_ref, v_ref):
    cq = cq_ref[...]
    ckv = ckv_ref[...]
    kpe = kpe_ref[...]
    cos = cos_ref[...]
    sin = sin_ref[...]
    scale = (B_NOPE + B_ROPE) ** -0.5
    for h in range(HEADS):
        qh = _dot(cq, wq_ref[:, h * B_QK_PAD:(h + 1) * B_QK_PAD])
        q_ref[:, h * B_QK_PAD:(h + 1) * B_QK_PAD] = (_rope(qh, cos, sin) * scale).astype(BF16)
        kh = _dot(ckv, wk_ref[:, h * B_NOPE:(h + 1) * B_NOPE])
        k_ref[:, h * B_QK_PAD:h * B_QK_PAD + B_NOPE] = kh.astype(BF16)
        k_ref[:, h * B_QK_PAD + B_NOPE:(h + 1) * B_QK_PAD] = kpe
    v_ref[...] = _dot(ckv, wv_ref[...]).astype(BF16)


def _mla_up(cq, ckv, kpe, wq, wk, wv, cos_q, sin_q):
    m = cq.shape[0]
    tm = TQ
    pos_blocks = SEQ // tm
    row = lambda i: (i, 0)
    fixed = lambda i: (0, 0)
    pos = lambda i: (i % pos_blocks, 0)
    return pl.pallas_call(
        _mla_up_kernel,
        grid=(m // tm,),
        in_specs=[
            pl.BlockSpec((tm, B_Q_RANK), row),
            pl.BlockSpec((tm, B_KV_RANK), row),
            pl.BlockSpec((tm, 128), row),
            pl.BlockSpec(wq.shape, fixed),
            pl.BlockSpec(wk.shape, fixed),
            pl.BlockSpec(wv.shape, fixed),
            pl.BlockSpec((tm, B_QK_PAD), pos),
            pl.BlockSpec((tm, B_QK_PAD), pos),
        ],
        out_specs=[
            pl.BlockSpec((tm, HEADS * B_QK_PAD), row),
            pl.BlockSpec((tm, HEADS * B_QK_PAD), row),
            pl.BlockSpec((tm, HEADS * HEAD_W), row),
        ],
        out_shape=[
            jax.ShapeDtypeStruct((m, HEADS * B_QK_PAD), BF16),
            jax.ShapeDtypeStruct((m, HEADS * B_QK_PAD), BF16),
            jax.ShapeDtypeStruct((m, HEADS * HEAD_W), BF16),
        ],
        compiler_params=pltpu.CompilerParams(
            dimension_semantics=("arbitrary",), vmem_limit_bytes=VMEM_LIMIT),
        name="mla_up",
    )(cq, ckv, kpe, wq, wk, wv, cos_q, sin_q)


def _gated_attn_kernel(q_ref, k_ref, v_ref, g_ref, o_ref):
    acc, l = _flash(q_ref[...], k_ref, v_ref)
    o_ref[...] = (acc / l * g_ref[...]).astype(BF16)


def _gated_attn(q, k, v, gate, *, d_qk, q_heads_per_kv, gate_block0, name):
    b, s, _ = q.shape
    grp = q_heads_per_kv
    return pl.pallas_call(
        _gated_attn_kernel,
        grid=(b, HEADS, s // TQ),
        in_specs=[
            pl.BlockSpec((None, TQ, d_qk), lambda bb, h, qi: (bb, qi, h)),
            pl.BlockSpec((None, s, d_qk), lambda bb, h, qi: (bb, 0, h // grp)),
            pl.BlockSpec((None, s, HEAD_W), lambda bb, h, qi: (bb, 0, h // grp)),
            pl.BlockSpec((None, TQ, HEAD_W), lambda bb, h, qi: (bb, qi, gate_block0 + h)),
        ],
        out_specs=pl.BlockSpec((None, TQ, HEAD_W), lambda bb, h, qi: (bb, qi, h)),
        out_shape=jax.ShapeDtypeStruct((b, s, HEADS * HEAD_W), BF16),
        compiler_params=pltpu.CompilerParams(
            dimension_semantics=("arbitrary",) * 3, vmem_limit_bytes=VMEM_LIMIT),
        name=name,
    )(q, k, v, gate)


def _outproj_kernel(oa_ref, ob_ref, w_ref, x_ref, pg_ref, y_ref):
    half = oa_ref.shape[-1]
    m = _dot(oa_ref[...], w_ref[:half, :]) + _dot(ob_ref[...], w_ref[half:, :])
    y_ref[...] = x_ref[...] + _rms(m, pg_ref[...])


def _outproj(oa, ob, w, x2, post_gain, name):
    m = x2.shape[0]
    tm = TM_PROJ
    row = lambda i: (i, 0)
    fixed = lambda i: (0, 0)
    return pl.pallas_call(
        _outproj_kernel,
        grid=(m // tm,),
        in_specs=[
            pl.BlockSpec((tm, oa.shape[-1]), row),
            pl.BlockSpec((tm, ob.shape[-1]), row),
            pl.BlockSpec(w.shape, fixed),
            pl.BlockSpec((tm, D_MODEL), row),
            pl.BlockSpec((1, D_MODEL), fixed),
        ],
        out_specs=pl.BlockSpec((tm, D_MODEL), row),
        out_shape=jax.ShapeDtypeStruct((m, D_MODEL), F32),
        compiler_params=pltpu.CompilerParams(
            dimension_semantics=("arbitrary",), vmem_limit_bytes=VMEM_LIMIT),
        name=name,
    )(oa, ob, w, x2, post_gain)


_O_QKV, _O_VD, _O_GATE, _O_QD, _O_KD = 0, 3072, 3328, 5376, 6400
_O_COLS = 6656


def _odd_inproj_kernel(x_ref, pg_ref, w_ref, qn_ref, kn_ref, cos_ref, sin_ref,
                       qkv_ref, vd_ref, gate_ref, qd_ref, kd_ref):
    h = _rms(x_ref[...], pg_ref[...]).astype(BF16)
    cos = cos_ref[...]
    sin = sin_ref[...]
    scale = HEAD_W ** -0.5

    def proj(c0, n):
        return _dot(h, w_ref[:, c0:c0 + n])

    for c in range(0, 1024, _N_CHUNK):
        qkv_ref[:, c:c + _N_CHUNK] = (proj(_O_QKV + c, _N_CHUNK) * scale).astype(BF16)
    for c in range(1024, 3072, _N_CHUNK):
        qkv_ref[:, c:c + _N_CHUNK] = proj(_O_QKV + c, _N_CHUNK).astype(BF16)
    vd_ref[...] = proj(_O_VD, 256).astype(BF16)
    for c in range(0, 2048, _N_CHUNK):
        gate_ref[:, c:c + _N_CHUNK] = _silu(proj(_O_GATE + c, _N_CHUNK))
    for c in range(0, 1024, _N_CHUNK):
        qd = proj(_O_QD + c, _N_CHUNK)
        for j in range(0, _N_CHUNK, HEAD_W):
            qh = _rope(_rms(qd[:, j:j + HEAD_W], qn_ref[...]), cos, sin) * scale
            qd_ref[:, c + j:c + j + HEAD_W] = qh.astype(BF16)
    kd = proj(_O_KD, 256)
    for j in range(0, 256, HEAD_W):
        kh = _rope(_rms(kd[:, j:j + HEAD_W], kn_ref[...]), cos, sin)
        kd_ref[:, j:j + HEAD_W] = kh.astype(BF16)


def _odd_inproj(x2, pre_gain, w_perm, q_norm, k_norm, cos_ax, sin_ax):
    m = x2.shape[0]
    tm = TM_PROJ
    pos_blocks = SEQ // tm
    row = lambda i: (i, 0)
    fixed = lambda i: (0, 0)
    pos = lambda i: (i % pos_blocks, 0)
    return pl.pallas_call(
        _odd_inproj_kernel,
        grid=(m // tm,),
        in_specs=[
            pl.BlockSpec((tm, D_MODEL), row),
            pl.BlockSpec((1, D_MODEL), fixed),
            pl.BlockSpec((D_MODEL, _O_COLS), fixed, pipeline_mode=pl.Buffered(1)),
            pl.BlockSpec((1, HEAD_W), fixed),
            pl.BlockSpec((1, HEAD_W), fixed),
            pl.BlockSpec((tm, HEAD_W), pos),
            pl.BlockSpec((tm, HEAD_W), pos),
        ],
        out_specs=[
            pl.BlockSpec((tm, 3072), row),
            pl.BlockSpec((tm, 256), row),
            pl.BlockSpec((tm, 2048), row),
            pl.BlockSpec((tm, 1024), row),
            pl.BlockSpec((tm, 256), row),
        ],
        out_shape=[
            jax.ShapeDtypeStruct((m, 3072), BF16),
            jax.ShapeDtypeStruct((m, 256), BF16),
            jax.ShapeDtypeStruct((m, 2048), F32),
            jax.ShapeDtypeStruct((m, 1024), BF16),
            jax.ShapeDtypeStruct((m, 256), BF16),
        ],
        compiler_params=pltpu.CompilerParams(
            dimension_semantics=("arbitrary",), vmem_limit_bytes=VMEM_LIMIT),
        name="odd_inproj",
    )(x2, pre_gain, w_perm, q_norm, k_norm, cos_ax, sin_ax)


_NA_Q = NA_GROUP_ROWS * GRID_W
_NA_K = NA_BAND_ROWS * GRID_W
_NA_GROUPS = SEQ // _NA_Q
_NA_LAST_START = SEQ // GRID_W - NA_BAND_ROWS


def _na_band_start_row(g):
    return jnp.clip(g * NA_GROUP_ROWS - NA_KH // 2, 0, _NA_LAST_START)


def _na_kernel(q_ref, k_ref, v_ref, bias_ref, g_ref, o_ref):
    g = pl.program_id(2)
    start = pl.multiple_of(_na_band_start_row(g) * GRID_W, GRID_W)
    s = _dot_nt(q_ref[...], k_ref[pl.ds(start, _NA_K), :]) + bias_ref[...]
    m = jnp.max(s, axis=1, keepdims=True)
    p = jnp.exp(s - m)
    l = jnp.sum(p, axis=1, keepdims=True)
    o = _dot(p.astype(BF16), v_ref[pl.ds(start, _NA_K), :]) / l
    o_ref[...] = (o * g_ref[...]).astype(BF16)


def _na_bias_class(g):
    return jnp.where(g == 0, 0, jnp.where(g == _NA_GROUPS - 1, 2, 1))


def _na_attn(qkv, gate, bias_table):
    b, s, _ = qkv.shape
    return pl.pallas_call(
        _na_kernel,
        grid=(b, HEADS, _NA_GROUPS),
        in_specs=[
            pl.BlockSpec((None, _NA_Q, HEAD_W), lambda bb, h, g: (bb, g, h)),
            pl.BlockSpec((None, s, HEAD_W), lambda bb, h, g: (bb, 0, HEADS + h)),
            pl.BlockSpec((None, s, HEAD_W), lambda bb, h, g: (bb, 0, 2 * HEADS + h)),
            pl.BlockSpec((None, None, _NA_Q, _NA_K), lambda bb, h, g: (_na_bias_class(g), h, 0, 0)),
            pl.BlockSpec((None, _NA_Q, HEAD_W), lambda bb, h, g: (bb, g, h)),
        ],
        out_specs=pl.BlockSpec((None, _NA_Q, HEAD_W), lambda bb, h, g: (bb, g, h)),
        out_shape=jax.ShapeDtypeStruct((b, s, HEADS * HEAD_W), BF16),
        compiler_params=pltpu.CompilerParams(
            dimension_semantics=("arbitrary",) * 3, vmem_limit_bytes=VMEM_LIMIT),
        name="na_attn",
    )(qkv, qkv, qkv, bias_table, gate)


def _na_bias_table(rpb):
    rows = SEQ // GRID_W
    c = np.arange(GRID_W)
    c0 = np.clip(c - NA_KW // 2, 0, GRID_W - NA_KW)
    col_in = (c[None, :] >= c0[:, None]) & (c[None, :] < c0[:, None] + NA_KW)
    dc = np.clip(c[None, :] - c[:, None], -(NA_KW - 1), NA_KW - 1) + NA_KW - 1
    tables = []
    for g in (0, 1, _NA_GROUPS - 1):
        start = int(np.clip(g * NA_GROUP_ROWS - NA_KH // 2, 0, _NA_LAST_START))
        rq = g * NA_GROUP_ROWS + np.arange(NA_GROUP_ROWS)
        rk = start + np.arange(NA_BAND_ROWS)
        r0 = np.clip(rq - NA_KH // 2, 0, rows - NA_KH)
        row_in = (rk[None, :] >= r0[:, None]) & (rk[None, :] < r0[:, None] + NA_KH)
        dr = np.clip(rk[None, :] - rq[:, None] + NA_KH - 1, 0, 2 * NA_KH - 2)
        bias = rpb[:, dr[:, None, :, None], dc[None, :, None, :]]
        valid = row_in[:, None, :, None] & col_in[None, :, None, :]
        bias = jnp.where(valid[None], bias, NEG_INF)
        tables.append(bias.reshape(HEADS, _NA_Q, _NA_K))
    return jnp.stack(tables)


def _rope_seg_tables(pos):
    inv = ROPE_THETA ** (-np.arange(ROPE_HALF, dtype=np.float64) / ROPE_HALF)
    ang = pos.astype(np.float64)[:, None] * inv[None, :]
    cos, sin = np.cos(ang), np.sin(ang)
    return np.concatenate([cos, cos], axis=1), np.concatenate([-sin, sin], axis=1)


def _rope_tables():
    t = np.arange(SEQ)
    cos_t, sin_t = _rope_seg_tables(t)
    ones, zeros = np.ones((SEQ, ROPE_SEG)), np.zeros((SEQ, ROPE_SEG))
    cos_k = np.concatenate([cos_t, ones], axis=1)
    sin_k = np.concatenate([sin_t, zeros], axis=1)
    cos_q = np.concatenate([ones, ones, cos_t, ones], axis=1)
    sin_q = np.concatenate([zeros, zeros, sin_t, zeros], axis=1)
    cos_r, sin_r = _rope_seg_tables(t // GRID_W)
    cos_c, sin_c = _rope_seg_tables(t % GRID_W)
    cos_ax = np.concatenate([cos_r, cos_c], axis=1)
    sin_ax = np.concatenate([sin_r, sin_c], axis=1)
    as_f32 = lambda a: jnp.asarray(a, F32)
    return tuple(map(as_f32, (cos_k, sin_k, cos_q, sin_q, cos_ax, sin_ax)))


def _even_weights(w_in, w_uq, w_ukv):
    qa, ka, va, ga, cq, ckv, kr, gb = jnp.split(
        w_in, np.cumsum((1024, 1024, 1024, 1024, B_Q_RANK, B_KV_RANK, B_ROPE))[:].tolist(), axis=1)
    w_perm = jnp.concatenate(
        [qa, ka, va, ga, gb, cq, ckv, kr, jnp.zeros((D_MODEL, 128 - B_ROPE), w_in.dtype)], axis=1)
    wq = w_uq.reshape(B_Q_RANK, HEADS, B_NOPE + B_ROPE)
    wq = jnp.pad(wq, ((0, 0), (0, 0), (0, B_QK_PAD - B_NOPE - B_ROPE))).reshape(B_Q_RANK, HEADS * B_QK_PAD)
    wkv = w_ukv.reshape(B_KV_RANK, HEADS, B_NOPE + HEAD_W)
    wk = wkv[:, :, :B_NOPE].reshape(B_KV_RANK, HEADS * B_NOPE)
    wv = wkv[:, :, B_NOPE:].reshape(B_KV_RANK, HEADS * HEAD_W)
    return w_perm.astype(BF16), wq.astype(BF16), wk.astype(BF16), wv.astype(BF16)


def _odd_weights(w_in):
    qc, kc, vc, gc, qd, kd, vd, gd = jnp.split(
        w_in, np.cumsum((1024, 1024, 1024, 1024, 1024, 256, 256)).tolist(), axis=1)
    return jnp.concatenate([qc, kc, vc, vd, gc, gd, qd, kd], axis=1).astype(BF16)


def _even_layer(x2, batch, layer, pre_gain, post_gain, w_in, w_out, lq1, lk1, lq2, lk2, subln,
                q_norm, w_uq, kv_norm, w_ukv, tables):
    cos_k, sin_k, cos_q, sin_q, _, _ = tables
    lambda_init = 0.8 - 0.6 * math.exp(-0.3 * layer)
    w_perm, wq, wk, wv = _even_weights(w_in, w_uq, w_ukv)
    qkv, gate, cq, ckv, kpe = _even_inproj(
        x2, pre_gain[None], w_perm, q_norm[None], kv_norm[None], cos_k, sin_k)
    as3 = lambda a: a.reshape(batch, SEQ, a.shape[-1])
    slopes = jnp.asarray(2.0 ** (-8.0 * np.arange(1, HEADS + 1) / HEADS), F32)
    oa = _diff_attn(slopes, as3(qkv), as3(gate), lq1[None], lk1[None], lq2[None], lk2[None],
                    subln[None], lambda_init)
    q_b, k_b, v_b = _mla_up(cq, ckv, kpe, wq, wk, wv, cos_q, sin_q)
    ob = _gated_attn(as3(q_b), as3(k_b), as3(v_b), as3(gate), d_qk=B_QK_PAD, q_heads_per_kv=1,
                     gate_block0=HEADS, name="latent_attn")
    m = x2.shape[0]
    return _outproj(oa.reshape(m, -1), ob.reshape(m, -1), w_out.astype(BF16), x2, post_gain[None],
                    "even_outproj")


def _odd_layer(x2, batch, pre_gain, post_gain, w_in, w_out, rpb, q_norm, k_norm, tables):
    cos_ax, sin_ax = tables[4], tables[5]
    qkv, vd, gate, qd, kd = _odd_inproj(
        x2, pre_gain[None], _odd_weights(w_in), q_norm[None], k_norm[None], cos_ax, sin_ax)
    as3 = lambda a: a.reshape(batch, SEQ, a.shape[-1])
    oc = _na_attn(as3(qkv), as3(gate), _na_bias_table(rpb))
    od = _gated_attn(as3(qd), as3(kd), as3(vd), as3(gate), d_qk=HEAD_W,
                     q_heads_per_kv=HEADS // D_KV_HEADS, gate_block0=HEADS, name="gqa_attn")
    m = x2.shape[0]
    return _outproj(oc.reshape(m, -1), od.reshape(m, -1), w_out.astype(BF16), x2, post_gain[None],
                    "odd_outproj")


def kernel(x, pre_norm, post_norm, even_w_in, even_w_out, diff_lambda_q1, diff_lambda_k1,
           diff_lambda_q2, diff_lambda_k2, diff_subln, mla_q_norm, mla_w_uq, mla_kv_norm, mla_w_ukv,
           odd_w_in, odd_w_out, na_rpb, gqa_q_norm, gqa_k_norm):
    batch, seq, d_model = x.shape
    assert (seq, d_model) == (SEQ, D_MODEL)
    depth = pre_norm.shape[0]
    tables = _rope_tables()
    x2 = x.reshape(batch * seq, d_model)
    for layer in range(depth):
        i = layer // 2
        if layer % 2 == 0:
            x2 = _even_layer(x2, batch, layer, pre_norm[layer], post_norm[layer], even_w_in[i],
                             even_w_out[i], diff_lambda_q1[i], diff_lambda_k1[i], diff_lambda_q2[i],
                             diff_lambda_k2[i], diff_subln[i], mla_q_norm[i], mla_w_uq[i],
                             mla_kv_norm[i], mla_w_ukv[i], tables)
        else:
            x2 = _odd_layer(x2, batch, pre_norm[layer], post_norm[layer], odd_w_in[i], odd_w_out[i],
                            na_rpb[i], gqa_q_norm[i], gqa_k_norm[i], tables)
    return x2.reshape(batch, seq, d_model)
```

```python
import functools
import math

import numpy as np
import jax
import jax.numpy as jnp
from jax import lax
from jax.experimental import pallas as pl
from jax.experimental.pallas import tpu as pltpu

F32 = jnp.float32
BF16 = jnp.bfloat16

D_MODEL = 2048
SEQ = 4096
GRID_W = 64
NORM_EPS = 1e-6
ROPE_THETA = 10000.0
NEG_INF = -1e30
LOG2E = math.log2(math.e)

HEADS = 8
HEAD_W = 128
A_DQK = 64
B_NOPE = 128
B_ROPE = 64
B_Q_RANK = 768
B_KV_RANK = 512
B_QK_PAD = 256
D_KV_HEADS = 2
NA_KH = 8
NA_KW = 16
NA_GROUP_ROWS = 4
NA_BAND_ROWS = 12

ROPE_SEG = 64
ROPE_HALF = 32

VMEM_LIMIT = 56 * 1024 * 1024

TM_PROJ = 256
TQ = 512
TK = 512
N_CHUNKS = SEQ // TK


def _rms(xf, gain):
    ms = jnp.mean(xf * xf, axis=-1, keepdims=True)
    return xf * lax.rsqrt(ms + NORM_EPS) * gain


def _silu(g):
    return g * (1.0 / (1.0 + jnp.exp(-g)))


def _rope(x, cos, sin_signed):
    w = x.shape[-1]
    lane = lax.broadcasted_iota(jnp.int32, x.shape, 1)
    first_half = (lane & (ROPE_SEG - 1)) < ROPE_HALF
    partner = jnp.where(first_half, pltpu.roll(x, w - ROPE_HALF, 1), pltpu.roll(x, ROPE_HALF, 1))
    return x * cos + partner * sin_signed


def _dot(a, b):
    return jnp.dot(a, b, preferred_element_type=F32)


def _dot_nt(a, b):
    return lax.dot_general(a, b, (((1,), (1,)), ((), ())), preferred_element_type=F32)


def _flash_init(tq):
    return (jnp.full((1, tq), NEG_INF, F32), jnp.zeros((1, tq), F32), jnp.zeros((HEAD_W, tq), F32))


def _flash_step(carry, st, vt):
    m, l, acc = carry
    m_new = jnp.maximum(m, jnp.max(st, axis=0, keepdims=True))
    alpha = jnp.exp2(m - m_new)
    p = jnp.exp2(st - m_new)
    l = alpha * l + jnp.sum(p, axis=0, keepdims=True)
    acc = alpha * acc + _dot(vt, p.astype(BF16))
    return m_new, l, acc


def _flash_run(n_streams, score_fn, vt_fn):
    carries = [_flash_init(TQ) for _ in range(n_streams)]
    st = [score_fn(0, j) for j in range(n_streams)]
    for step in range(N_CHUNKS):
        st_next = [score_fn(step + 1, j) for j in range(n_streams)] if step + 1 < N_CHUNKS else None
        for j in range(n_streams):
            carries[j] = _flash_step(carries[j], st[j], vt_fn(step, j))
        st = st_next
    return carries


def _flash_finish(carry):
    _, l, acc = carry
    return (acc * (1.0 / l)).T


_E_QK, _E_GATE, _E_CQ, _E_CKV, _E_KR = 0, 2048, 4096, 4864, 5376
_E_COLS = 5504
_N_CHUNK = 512


def _even_inproj_kernel(x_ref, pg_ref, w_ref, wvt_ref, qn_ref, kvn_ref, cos_ref, sin_ref,
                        qk_ref, vt_ref, gate_ref, cq_ref, ckv_ref, kpe_ref):
    h = _rms(x_ref[...], pg_ref[...]).astype(BF16)

    def proj(c0, n):
        return _dot(h, w_ref[:, c0:c0 + n])

    q_scale = (A_DQK ** -0.5) * LOG2E
    for c in range(0, 1024, _N_CHUNK):
        qk_ref[:, c:c + _N_CHUNK] = (proj(_E_QK + c, _N_CHUNK) * q_scale).astype(BF16)
    for c in range(1024, 2048, _N_CHUNK):
        qk_ref[:, c:c + _N_CHUNK] = proj(_E_QK + c, _N_CHUNK).astype(BF16)
    for c in range(0, 1024, _N_CHUNK):
        vt_ref[c:c + _N_CHUNK, :] = _dot_nt(wvt_ref[c:c + _N_CHUNK, :], h).astype(BF16)
    for c in range(0, 2048, _N_CHUNK):
        gate_ref[:, c:c + _N_CHUNK] = _silu(proj(_E_GATE + c, _N_CHUNK))
    cq_ref[...] = _rms(proj(_E_CQ, B_Q_RANK), qn_ref[...]).astype(BF16)
    ckv_ref[...] = _rms(proj(_E_CKV, B_KV_RANK), kvn_ref[...]).astype(BF16)
    kpe_ref[...] = _rope(proj(_E_KR, 128), cos_ref[...], sin_ref[...]).astype(BF16)


def _vt_out_spec(rows, tm):
    per_chunk = TK // tm
    return pl.BlockSpec((None, rows, tm), lambda i: (i // per_chunk, 0, i % per_chunk))


def _even_inproj(x2, pre_gain, w_perm, wvt, q_norm, kv_norm, cos_k, sin_k):
    m = x2.shape[0]
    tm = TM_PROJ
    pos_blocks = SEQ // tm
    row = lambda i: (i, 0)
    fixed = lambda i: (0, 0)
    pos = lambda i: (i % pos_blocks, 0)
    return pl.pallas_call(
        _even_inproj_kernel,
        grid=(m // tm,),
        in_specs=[
            pl.BlockSpec((tm, D_MODEL), row),
            pl.BlockSpec((1, D_MODEL), fixed),
            pl.BlockSpec((D_MODEL, _E_COLS), fixed, pipeline_mode=pl.Buffered(1)),
            pl.BlockSpec((1024, D_MODEL), fixed, pipeline_mode=pl.Buffered(1)),
            pl.BlockSpec((1, B_Q_RANK), fixed),
            pl.BlockSpec((1, B_KV_RANK), fixed),
            pl.BlockSpec((tm, 128), pos),
            pl.BlockSpec((tm, 128), pos),
        ],
        out_specs=[
            pl.BlockSpec((tm, 2048), row),
            _vt_out_spec(1024, tm),
            pl.BlockSpec((tm, 2048), row),
            pl.BlockSpec((tm, B_Q_RANK), row),
            pl.BlockSpec((tm, B_KV_RANK), row),
            pl.BlockSpec((tm, 128), row),
        ],
        out_shape=[
            jax.ShapeDtypeStruct((m, 2048), BF16),
            jax.ShapeDtypeStruct((m // TK, 1024, TK), BF16),
            jax.ShapeDtypeStruct((m, 2048), F32),
            jax.ShapeDtypeStruct((m, B_Q_RANK), BF16),
            jax.ShapeDtypeStruct((m, B_KV_RANK), BF16),
            jax.ShapeDtypeStruct((m, 128), BF16),
        ],
        compiler_params=pltpu.CompilerParams(
            dimension_semantics=("arbitrary",), vmem_limit_bytes=VMEM_LIMIT),
        name="even_inproj",
    )(x2, pre_gain, w_perm, wvt, q_norm, kv_norm, cos_k, sin_k)


def _log2e_pieces():
    pieces, rest = [], np.float64(LOG2E)
    for _ in range(3):
        p = np.float64(np.asarray(rest).astype(BF16))
        pieces.append(p)
        rest = rest - p
    return pieces


def _alibi_aug_tables():
    pos = np.arange(SEQ)
    hi, lo = (pos // 64) * 64.0, (pos % 64) * 1.0
    c = _log2e_pieces()
    qaug = np.zeros((SEQ, HEAD_W))
    kaug = np.zeros((SEQ, HEAD_W))
    for p in range(3):
        qaug[:, p], kaug[:, p] = c[p], hi
        qaug[:, 3 + p], kaug[:, 3 + p] = c[p], lo
        qaug[:, 6 + p], kaug[:, 6 + p] = -hi, c[p]
        qaug[:, 9 + p], kaug[:, 9 + p] = -lo, c[p]
    return jnp.asarray(qaug, F32), jnp.asarray(kaug, BF16)


def _diff_attn_kernel(slopes_ref, q_ref, k_ref, vt_ref, g_ref, qaug_ref, kaug_ref,
                      lq1_ref, lk1_ref, lq2_ref, lk2_ref, subln_ref, o_ref, kcat_ref, *, lambda_init):
    head = pl.program_id(1)
    qi = pl.program_id(2)
    slope = slopes_ref[head]

    @pl.when(qi == 0)
    def _():
        k = k_ref[...]
        kaug = kaug_ref[...]
        kcat_ref[0, :, :HEAD_W] = k
        kcat_ref[0, :, HEAD_W:] = kaug
        kcat_ref[1, :, :HEAD_W] = k
        kcat_ref[1, :, HEAD_W:] = -kaug

    q = q_ref[...]
    lane = lax.broadcasted_iota(jnp.int32, q.shape, 1)
    zero = jnp.zeros_like(q)
    q_maps = (jnp.where(lane < A_DQK, q, zero), jnp.where(lane >= A_DQK, q, zero))
    qaug = (qaug_ref[...] * slope).astype(BF16)
    q_diag = [jnp.concatenate([qm, zero], axis=1) for qm in q_maps]
    q_off = [jnp.concatenate([qm, qaug], axis=1) for qm in q_maps]

    dist = jnp.abs(lax.broadcasted_iota(jnp.int32, (TK, TQ), 0)
                   - lax.broadcasted_iota(jnp.int32, (TK, TQ), 1)).astype(F32)
    diag_bias = dist * (-slope * LOG2E)

    def chunk_of(r):
        wrapped = (qi + r >= N_CHUNKS).astype(jnp.int32)
        return qi + r - wrapped * N_CHUNKS, wrapped

    def score(r, mi):
        c, wrapped = chunk_of(r)
        kc = kcat_ref[1 - wrapped, pl.ds(pl.multiple_of(c * TK, TK), TK), :]
        if r == 0:
            return _dot_nt(kc, q_diag[mi]) + diag_bias
        return _dot_nt(kc, q_off[mi])

    def values(r, mi):
        return vt_ref[chunk_of(r)[0]]

    carries = _flash_run(2, score, values)

    lam = (jnp.exp(jnp.sum(lq1_ref[...] * lk1_ref[...], axis=-1, keepdims=True))
           - jnp.exp(jnp.sum(lq2_ref[...] * lk2_ref[...], axis=-1, keepdims=True))
           + lambda_init)
    o = _flash_finish(carries[0]) - lam * _flash_finish(carries[1])
    o = _rms(o, subln_ref[...]) * (1.0 - lambda_init)
    o_ref[...] = (o * g_ref[...]).astype(BF16)


def _diff_attn(slopes, qk, vt, gate, qaug, kaug, lq1, lk1, lq2, lk2, subln, lambda_init):
    b, s, _ = qk.shape
    vec = lambda bb, h, qi: (0, 0)
    return pl.pallas_call(
        functools.partial(_diff_attn_kernel, lambda_init=lambda_init),
        grid=(b, HEADS, s // TQ),
        in_specs=[
            pl.BlockSpec(memory_space=pltpu.SMEM),
            pl.BlockSpec((None, TQ, HEAD_W), lambda bb, h, qi: (bb, qi, h)),
            pl.BlockSpec((None, s, HEAD_W), lambda bb, h, qi: (bb, 0, HEADS + h)),
            pl.BlockSpec((N_CHUNKS, HEAD_W, TK), lambda bb, h, qi: (bb, h, 0)),
            pl.BlockSpec((None, TQ, HEAD_W), lambda bb, h, qi: (bb, qi, h)),
            pl.BlockSpec((TQ, HEAD_W), lambda bb, h, qi: (qi, 0)),
            pl.BlockSpec((s, HEAD_W), vec),
            pl.BlockSpec((1, A_DQK), vec),
            pl.BlockSpec((1, A_DQK), vec),
            pl.BlockSpec((1, A_DQK), vec),
            pl.BlockSpec((1, A_DQK), vec),
            pl.BlockSpec((1, HEAD_W), vec),
        ],
        out_specs=pl.BlockSpec((None, TQ, HEAD_W), lambda bb, h, qi: (bb, qi, h)),
        out_shape=jax.ShapeDtypeStruct((b, s, HEADS * HEAD_W), BF16),
        scratch_shapes=[pltpu.VMEM((2, s, 2 * HEAD_W), BF16)],
        compiler_params=pltpu.CompilerParams(
            dimension_semantics=("arbitrary",) * 3, vmem_limit_bytes=VMEM_LIMIT),
        name="diff_attn",
    )(slopes, qk, qk, vt, gate, qaug, kaug, lq1, lk1, lq2, lk2, subln)


def _mla_up_kernel(cq_ref, ckv_ref, kpe_ref, wq_ref, wk_ref, wvt_ref, cos_ref, sin_ref,
                   q_ref, k_ref, vt_ref):
    cq = cq_ref[...]
    ckv = ckv_ref[...]
    kpe = kpe_ref[...]
    cos = cos_ref[...]
    sin = sin_ref[...]
    scale = (B_NOPE + B_ROPE) ** -0.5 * LOG2E
    for h in range(HEADS):
        qh = _dot(cq, wq_ref[:, h * B_QK_PAD:(h + 1) * B_QK_PAD])
        q_ref[:, h * B_QK_PAD:(h + 1) * B_QK_PAD] = (_rope(qh, cos, sin) * scale).astype(BF16)
        kh = _dot(ckv, wk_ref[:, h * B_NOPE:(h + 1) * B_NOPE])
        k_ref[:, h * B_QK_PAD:h * B_QK_PAD + B_NOPE] = kh.astype(BF16)
        k_ref[:, h * B_QK_PAD + B_NOPE:(h + 1) * B_QK_PAD] = kpe
    for c in range(0, 1024, _N_CHUNK):
        vt_ref[c:c + _N_CHUNK, :] = _dot_nt(wvt_ref[c:c + _N_CHUNK, :], ckv).astype(BF16)


def _mla_up(cq, ckv, kpe, wq, wk, wvt, cos_q, sin_q):
    m = cq.shape[0]
    tm = TK
    pos_blocks = SEQ // tm
    row = lambda i: (i, 0)
    fixed = lambda i: (0, 0)
    pos = lambda i: (i % pos_blocks, 0)
    return pl.pallas_call(
        _mla_up_kernel,
        grid=(m // tm,),
        in_specs=[
            pl.BlockSpec((tm, B_Q_RANK), row),
            pl.BlockSpec((tm, B_KV_RANK), row),
            pl.BlockSpec((tm, 128), row),
            pl.BlockSpec(wq.shape, fixed),
            pl.BlockSpec(wk.shape, fixed),
            pl.BlockSpec(wvt.shape, fixed),
            pl.BlockSpec((tm, B_QK_PAD), pos),
            pl.BlockSpec((tm, B_QK_PAD), pos),
        ],
        out_specs=[
            pl.BlockSpec((tm, HEADS * B_QK_PAD), row),
            pl.BlockSpec((tm, HEADS * B_QK_PAD), row),
            _vt_out_spec(1024, tm),
        ],
        out_shape=[
            jax.ShapeDtypeStruct((m, HEADS * B_QK_PAD), BF16),
            jax.ShapeDtypeStruct((m, HEADS * B_QK_PAD), BF16),
            jax.ShapeDtypeStruct((m // TK, 1024, TK), BF16),
        ],
        compiler_params=pltpu.CompilerParams(
            dimension_semantics=("arbitrary",), vmem_limit_bytes=VMEM_LIMIT),
        name="mla_up",
    )(cq, ckv, kpe, wq, wk, wvt, cos_q, sin_q)


def _gated_attn_kernel(q_ref, k_ref, vt_ref, g_ref, o_ref, *, d_qk, heads_per_step, kv_per_step):
    qs = [q_ref[:, j * d_qk:(j + 1) * d_qk] for j in range(heads_per_step)]
    kv_of = lambda j: j if kv_per_step > 1 else 0

    def score(c, j):
        jk = kv_of(j)
        return _dot_nt(k_ref[c * TK:(c + 1) * TK, jk * d_qk:(jk + 1) * d_qk], qs[j])

    def values(c, j):
        jk = kv_of(j)
        return vt_ref[c, jk * HEAD_W:(jk + 1) * HEAD_W, :]

    carries = _flash_run(heads_per_step, score, values)
    for j in range(heads_per_step):
        o = _flash_finish(carries[j]) * g_ref[:, j * HEAD_W:(j + 1) * HEAD_W]
        o_ref[:, j * HEAD_W:(j + 1) * HEAD_W] = o.astype(BF16)


def _gated_attn(q, k, vt, gate, *, d_qk, q_heads_per_kv, heads_per_step, gate_block0, name):
    b, s, _ = q.shape
    hs = heads_per_step
    kv_per_step = max(1, hs // q_heads_per_kv)
    steps_per_kv = max(1, q_heads_per_kv // hs)
    gate0 = gate_block0 // hs
    return pl.pallas_call(
        functools.partial(_gated_attn_kernel, d_qk=d_qk, heads_per_step=hs, kv_per_step=kv_per_step),
        grid=(b, HEADS // hs, s // TQ),
        in_specs=[
            pl.BlockSpec((None, TQ, hs * d_qk), lambda bb, h, qi: (bb, qi, h)),
            pl.BlockSpec((None, s, kv_per_step * d_qk), lambda bb, h, qi: (bb, 0, h // steps_per_kv)),
            pl.BlockSpec((N_CHUNKS, kv_per_step * HEAD_W, TK), lambda bb, h, qi: (bb, h // steps_per_kv, 0)),
            pl.BlockSpec((None, TQ, hs * HEAD_W), lambda bb, h, qi: (bb, qi, gate0 + h)),
        ],
        out_specs=pl.BlockSpec((None, TQ, hs * HEAD_W), lambda bb, h, qi: (bb, qi, h)),
        out_shape=jax.ShapeDtypeStruct((b, s, HEADS * HEAD_W), BF16),
        compiler_params=pltpu.CompilerParams(
            dimension_semantics=("arbitrary",) * 3, vmem_limit_bytes=VMEM_LIMIT),
        name=name,
    )(q, k, vt, gate)


def _outproj_kernel(oa_ref, ob_ref, w_ref, x_ref, pg_ref, y_ref):
    half = oa_ref.shape[-1]
    m = _dot(oa_ref[...], w_ref[:half, :]) + _dot(ob_ref[...], w_ref[half:, :])
    y_ref[...] = x_ref[...] + _rms(m, pg_ref[...])


def _outproj(oa, ob, w, x2, post_gain, name):
    m = x2.shape[0]
    tm = TM_PROJ
    row = lambda i: (i, 0)
    fixed = lambda i: (0, 0)
    return pl.pallas_call(
        _outproj_kernel,
        grid=(m // tm,),
        in_specs=[
            pl.BlockSpec((tm, oa.shape[-1]), row),
            pl.BlockSpec((tm, ob.shape[-1]), row),
            pl.BlockSpec(w.shape, fixed),
            pl.BlockSpec((tm, D_MODEL), row),
            pl.BlockSpec((1, D_MODEL), fixed),
        ],
        out_specs=pl.BlockSpec((tm, D_MODEL), row),
        out_shape=jax.ShapeDtypeStruct((m, D_MODEL), F32),
        compiler_params=pltpu.CompilerParams(
            dimension_semantics=("arbitrary",), vmem_limit_bytes=VMEM_LIMIT),
        name=name,
    )(oa, ob, w, x2, post_gain)


_O_QKV, _O_GATE, _O_QD, _O_KD = 0, 3072, 5120, 6144
_O_COLS = 6400


def _odd_inproj_kernel(x_ref, pg_ref, w_ref, wvt_ref, qn_ref, kn_ref, cos_ref, sin_ref,
                       qkv_ref, vt_ref, gate_ref, qd_ref, kd_ref):
    h = _rms(x_ref[...], pg_ref[...]).astype(BF16)
    cos = cos_ref[...]
    sin = sin_ref[...]
    scale = HEAD_W ** -0.5 * LOG2E

    def proj(c0, n):
        return _dot(h, w_ref[:, c0:c0 + n])

    for c in range(0, 1024, _N_CHUNK):
        qkv_ref[:, c:c + _N_CHUNK] = (proj(_O_QKV + c, _N_CHUNK) * scale).astype(BF16)
    for c in range(1024, 3072, _N_CHUNK):
        qkv_ref[:, c:c + _N_CHUNK] = proj(_O_QKV + c, _N_CHUNK).astype(BF16)
    vt_ref[...] = _dot_nt(wvt_ref[...], h).astype(BF16)
    for c in range(0, 2048, _N_CHUNK):
        gate_ref[:, c:c + _N_CHUNK] = _silu(proj(_O_GATE + c, _N_CHUNK))
    for c in range(0, 1024, _N_CHUNK):
        qd = proj(_O_QD + c, _N_CHUNK)
        for j in range(0, _N_CHUNK, HEAD_W):
            qh = _rope(_rms(qd[:, j:j + HEAD_W], qn_ref[...]), cos, sin) * scale
            qd_ref[:, c + j:c + j + HEAD_W] = qh.astype(BF16)
    kd = proj(_O_KD, 256)
    for j in range(0, 256, HEAD_W):
        kh = _rope(_rms(kd[:, j:j + HEAD_W], kn_ref[...]), cos, sin)
        kd_ref[:, j:j + HEAD_W] = kh.astype(BF16)


def _odd_inproj(x2, pre_gain, w_perm, wvt, q_norm, k_norm, cos_ax, sin_ax):
    m = x2.shape[0]
    tm = TM_PROJ
    pos_blocks = SEQ // tm
    row = lambda i: (i, 0)
    fixed = lambda i: (0, 0)
    pos = lambda i: (i % pos_blocks, 0)
    return pl.pallas_call(
        _odd_inproj_kernel,
        grid=(m // tm,),
        in_specs=[
            pl.BlockSpec((tm, D_MODEL), row),
            pl.BlockSpec((1, D_MODEL), fixed),
            pl.BlockSpec((D_MODEL, _O_COLS), fixed, pipeline_mode=pl.Buffered(1)),
            pl.BlockSpec((256, D_MODEL), fixed, pipeline_mode=pl.Buffered(1)),
            pl.BlockSpec((1, HEAD_W), fixed),
            pl.BlockSpec((1, HEAD_W), fixed),
            pl.BlockSpec((tm, HEAD_W), pos),
            pl.BlockSpec((tm, HEAD_W), pos),
        ],
        out_specs=[
            pl.BlockSpec((tm, 3072), row),
            _vt_out_spec(256, tm),
            pl.BlockSpec((tm, 2048), row),
            pl.BlockSpec((tm, 1024), row),
            pl.BlockSpec((tm, 256), row),
        ],
        out_shape=[
            jax.ShapeDtypeStruct((m, 3072), BF16),
            jax.ShapeDtypeStruct((m // TK, 256, TK), BF16),
            jax.ShapeDtypeStruct((m, 2048), F32),
            jax.ShapeDtypeStruct((m, 1024), BF16),
            jax.ShapeDtypeStruct((m, 256), BF16),
        ],
        compiler_params=pltpu.CompilerParams(
            dimension_semantics=("arbitrary",), vmem_limit_bytes=VMEM_LIMIT),
        name="odd_inproj",
    )(x2, pre_gain, w_perm, wvt, q_norm, k_norm, cos_ax, sin_ax)


_NA_Q = NA_GROUP_ROWS * GRID_W
_NA_K = NA_BAND_ROWS * GRID_W
_NA_GROUPS = SEQ // _NA_Q
_NA_ROWS = SEQ // GRID_W
_NA_LAST_START = _NA_ROWS - NA_BAND_ROWS
_NA_CLASS_GROUPS = (0, 1, _NA_GROUPS - 1)


def _na_band_start(g):
    lo = g * NA_GROUP_ROWS - NA_KH // 2
    if isinstance(g, int):
        return min(max(lo, 0), _NA_LAST_START)
    return jnp.clip(lo, 0, _NA_LAST_START)


def _na_build_bias(rpb_ref, bias_ref):
    vec = rpb_ref[...] * LOG2E
    even_src = pltpu.roll(vec, 128 - (NA_KW - 1), 1)
    odd_src = pltpu.roll(vec, GRID_W - (NA_KW - 1), 1)
    q_idx = lax.broadcasted_iota(jnp.int32, (GRID_W, 128), 0)
    lane = lax.broadcasted_iota(jnp.int32, (GRID_W, 128), 1)
    w_idx = lane & (GRID_W - 1)
    c0 = jnp.clip(q_idx - NA_KW // 2, 0, GRID_W - NA_KW)
    col_in = (w_idx >= c0) & (w_idx < c0 + NA_KW)
    neg = jnp.full((GRID_W, 128), NEG_INF, F32)
    cache = {}
    for cls, g in enumerate(_NA_CLASS_GROUPS):
        start = _na_band_start(g)
        for qr in range(NA_GROUP_ROWS):
            rq = g * NA_GROUP_ROWS + qr
            r0 = min(max(rq - NA_KH // 2, 0), _NA_ROWS - NA_KH)
            for kp in range(NA_BAND_ROWS // 2):
                halves = []
                for rk in (start + 2 * kp, start + 2 * kp + 1):
                    halves.append(rk - rq + NA_KH - 1 if r0 <= rk < r0 + NA_KH else None)
                key = tuple(halves)
                if key not in cache:
                    if halves[0] is None and halves[1] is None:
                        cache[key] = neg
                    else:
                        src = jnp.zeros((1, 128), F32)
                        mask = None
                        if halves[0] is not None:
                            src = src + even_src[halves[0]:halves[0] + 1, :]
                            mask = lane < GRID_W
                        if halves[1] is not None:
                            src = src + odd_src[halves[1]:halves[1] + 1, :]
                            mask = (lane >= GRID_W) if mask is None else None
                        tile = pltpu.roll(jnp.broadcast_to(src, (GRID_W, 128)), 0, 1, stride=1, stride_axis=0)
                        keep = col_in if mask is None else (col_in & mask)
                        cache[key] = jnp.where(keep, tile, neg)
                bias_ref[cls, qr * GRID_W:(qr + 1) * GRID_W, kp * 128:(kp + 1) * 128] = cache[key]


def _na_kernel(q_ref, k_ref, v_ref, rpb_ref, g_ref, o_ref, bias_ref):
    g = pl.program_id(2)

    @pl.when(g == 0)
    def _():
        _na_build_bias(rpb_ref, bias_ref)

    cls = jnp.where(g == 0, 0, jnp.where(g == _NA_GROUPS - 1, 2, 1))
    start = pl.multiple_of(_na_band_start(g) * GRID_W, GRID_W)
    s = _dot_nt(q_ref[...], k_ref[pl.ds(start, _NA_K), :]) + bias_ref[cls]
    m = jnp.max(s, axis=1, keepdims=True)
    p = jnp.exp2(s - m)
    l = jnp.sum(p, axis=1, keepdims=True)
    o = _dot(p.astype(BF16), v_ref[pl.ds(start, _NA_K), :]) * (1.0 / l)
    o_ref[...] = (o * g_ref[...]).astype(BF16)


def _na_attn(qkv, gate, rpb_pad):
    b, s, _ = qkv.shape
    return pl.pallas_call(
        _na_kernel,
        grid=(b, HEADS, _NA_GROUPS),
        in_specs=[
            pl.BlockSpec((None, _NA_Q, HEAD_W), lambda bb, h, g: (bb, g, h)),
            pl.BlockSpec((None, s, HEAD_W), lambda bb, h, g: (bb, 0, HEADS + h)),
            pl.BlockSpec((None, s, HEAD_W), lambda bb, h, g: (bb, 0, 2 * HEADS + h)),
            pl.BlockSpec((None, 16, 128), lambda bb, h, g: (h, 0, 0)),
            pl.BlockSpec((None, _NA_Q, HEAD_W), lambda bb, h, g: (bb, g, h)),
        ],
        out_specs=pl.BlockSpec((None, _NA_Q, HEAD_W), lambda bb, h, g: (bb, g, h)),
        out_shape=jax.ShapeDtypeStruct((b, s, HEADS * HEAD_W), BF16),
        scratch_shapes=[pltpu.VMEM((3, _NA_Q, _NA_K), F32)],
        compiler_params=pltpu.CompilerParams(
            dimension_semantics=("arbitrary",) * 3, vmem_limit_bytes=VMEM_LIMIT),
        name="na_attn",
    )(qkv, qkv, qkv, rpb_pad, gate)


def _rope_seg_tables(pos):
    inv = ROPE_THETA ** (-np.arange(ROPE_HALF, dtype=np.float64) / ROPE_HALF)
    ang = pos.astype(np.float64)[:, None] * inv[None, :]
    cos, sin = np.cos(ang), np.sin(ang)
    return np.concatenate([cos, cos], axis=1), np.concatenate([-sin, sin], axis=1)


def _rope_tables():
    t = np.arange(SEQ)
    cos_t, sin_t = _rope_seg_tables(t)
    ones, zeros = np.ones((SEQ, ROPE_SEG)), np.zeros((SEQ, ROPE_SEG))
    cos_k = np.concatenate([cos_t, ones], axis=1)
    sin_k = np.concatenate([sin_t, zeros], axis=1)
    cos_q = np.concatenate([ones, ones, cos_t, ones], axis=1)
    sin_q = np.concatenate([zeros, zeros, sin_t, zeros], axis=1)
    cos_r, sin_r = _rope_seg_tables(t // GRID_W)
    cos_c, sin_c = _rope_seg_tables(t % GRID_W)
    cos_ax = np.concatenate([cos_r, cos_c], axis=1)
    sin_ax = np.concatenate([sin_r, sin_c], axis=1)
    as_f32 = lambda a: jnp.asarray(a, F32)
    return tuple(map(as_f32, (cos_k, sin_k, cos_q, sin_q, cos_ax, sin_ax)))


def _even_weights(w_in, w_uq, w_ukv):
    qa, ka, va, ga, cq, ckv, kr, gb = jnp.split(
        w_in, np.cumsum((1024, 1024, 1024, 1024, B_Q_RANK, B_KV_RANK, B_ROPE)).tolist(), axis=1)
    w_perm = jnp.concatenate(
        [qa, ka, ga, gb, cq, ckv, kr, jnp.zeros((D_MODEL, 128 - B_ROPE), w_in.dtype)], axis=1)
    wq = w_uq.reshape(B_Q_RANK, HEADS, B_NOPE + B_ROPE)
    wq = jnp.pad(wq, ((0, 0), (0, 0), (0, B_QK_PAD - B_NOPE - B_ROPE))).reshape(B_Q_RANK, HEADS * B_QK_PAD)
    wkv = w_ukv.reshape(B_KV_RANK, HEADS, B_NOPE + HEAD_W)
    wk = wkv[:, :, :B_NOPE].reshape(B_KV_RANK, HEADS * B_NOPE)
    wv = wkv[:, :, B_NOPE:].reshape(B_KV_RANK, HEADS * HEAD_W)
    return (w_perm.astype(BF16), va.T.astype(BF16), wq.astype(BF16), wk.astype(BF16), wv.T.astype(BF16))


def _odd_weights(w_in):
    qc, kc, vc, gc, qd, kd, vd, gd = jnp.split(
        w_in, np.cumsum((1024, 1024, 1024, 1024, 1024, 256, 256)).tolist(), axis=1)
    return jnp.concatenate([qc, kc, vc, gc, gd, qd, kd], axis=1).astype(BF16), vd.T.astype(BF16)


def _even_layer(x2, batch, layer, pre_gain, post_gain, w_in, w_out, lq1, lk1, lq2, lk2, subln,
                q_norm, w_uq, kv_norm, w_ukv, tables):
    cos_k, sin_k, cos_q, sin_q, _, _ = tables
    lambda_init = 0.8 - 0.6 * math.exp(-0.3 * layer)
    w_perm, wvat, wq, wk, wvbt = _even_weights(w_in, w_uq, w_ukv)
    qk, vat, gate, cq, ckv, kpe = _even_inproj(
        x2, pre_gain[None], w_perm, wvat, q_norm[None], kv_norm[None], cos_k, sin_k)
    as3 = lambda a: a.reshape(batch, SEQ, a.shape[-1])
    slopes = jnp.asarray(2.0 ** (-8.0 * np.arange(1, HEADS + 1) / HEADS), F32)
    qaug, kaug = _alibi_aug_tables()
    oa = _diff_attn(slopes, as3(qk), vat, as3(gate), qaug, kaug, lq1[None], lk1[None], lq2[None],
                    lk2[None], subln[None], lambda_init)
    q_b, k_b, vbt = _mla_up(cq, ckv, kpe, wq, wk, wvbt, cos_q, sin_q)
    ob = _gated_attn(as3(q_b), as3(k_b), vbt, as3(gate), d_qk=B_QK_PAD, q_heads_per_kv=1,
                     heads_per_step=2, gate_block0=HEADS, name="latent_attn")
    m = x2.shape[0]
    return _outproj(oa.reshape(m, -1), ob.reshape(m, -1), w_out.astype(BF16), x2, post_gain[None],
                    "even_outproj")


def _odd_layer(x2, batch, pre_gain, post_gain, w_in, w_out, rpb, q_norm, k_norm, tables):
    cos_ax, sin_ax = tables[4], tables[5]
    w_perm, wvdt = _odd_weights(w_in)
    qkv, vdt, gate, qd, kd = _odd_inproj(
        x2, pre_gain[None], w_perm, wvdt, q_norm[None], k_norm[None], cos_ax, sin_ax)
    as3 = lambda a: a.reshape(batch, SEQ, a.shape[-1])
    rpb_pad = jnp.pad(rpb, ((0, 0), (0, 16 - rpb.shape[1]), (0, 128 - rpb.shape[2])))
    oc = _na_attn(as3(qkv), as3(gate), rpb_pad)
    od = _gated_attn(as3(qd), as3(kd), vdt, as3(gate), d_qk=HEAD_W,
                     q_heads_per_kv=HEADS // D_KV_HEADS, heads_per_step=2, gate_block0=HEADS,
                     name="gqa_attn")
    m = x2.shape[0]
    return _outproj(oc.reshape(m, -1), od.reshape(m, -1), w_out.astype(BF16), x2, post_gain[None],
                    "odd_outproj")


def kernel(x, pre_norm, post_norm, even_w_in, even_w_out, diff_lambda_q1, diff_lambda_k1,
           diff_lambda_q2, diff_lambda_k2, diff_subln, mla_q_norm, mla_w_uq, mla_kv_norm, mla_w_ukv,
           odd_w_in, odd_w_out, na_rpb, gqa_q_norm, gqa_k_norm):
    batch, seq, d_model = x.shape
    assert (seq, d_model) == (SEQ, D_MODEL)
    depth = pre_norm.shape[0]
    tables = _rope_tables()
    x2 = x.reshape(batch * seq, d_model)
    for layer in range(depth):
        i = layer // 2
        if layer % 2 == 0:
            x2 = _even_layer(x2, batch, layer, pre_norm[layer], post_norm[layer], even_w_in[i],
                             even_w_out[i], diff_lambda_q1[i], diff_lambda_k1[i], diff_lambda_q2[i],
                             diff_lambda_k2[i], diff_subln[i], mla_q_norm[i], mla_w_uq[i],
                             mla_kv_norm[i], mla_w_ukv[i], tables)
        else:
            x2 = _odd_layer(x2, batch, pre_norm[layer], post_norm[layer], odd_w_in[i], odd_w_out[i],
                            na_rpb[i], gqa_q_norm[i], gqa_k_norm[i], tables)
    return x2.reshape(batch, seq, d_model)
```

```python
import functools
import math

import numpy as np
import jax
import jax.numpy as jnp
from jax import lax
from jax.experimental import pallas as pl
from jax.experimental.pallas import tpu as pltpu

F32 = jnp.float32
BF16 = jnp.bfloat16

D_MODEL = 2048
SEQ = 4096
GRID_W = 64
NORM_EPS = 1e-6
ROPE_THETA = 10000.0
NEG_INF = -1e30
LOG2E = math.log2(math.e)

HEADS = 8
HEAD_W = 128
V_ROWS = HEAD_W + 16
A_DQK = 64
B_NOPE = 128
B_ROPE = 64
B_Q_RANK = 768
B_KV_RANK = 512
B_QK_PAD = 256
D_KV_HEADS = 2
NA_KH = 8
NA_KW = 16
NA_GROUP_ROWS = 4
NA_BAND_ROWS = 12
NA_HEADS_PER_STEP = 4

ROPE_SEG = 64
ROPE_HALF = 32

VMEM_LIMIT = 56 * 1024 * 1024

TM_PROJ = 256
TQ = 512
TK = 512
N_CHUNKS = SEQ // TK
TK_PLAIN = 512


def _rms(xf, gain):
    ms = jnp.mean(xf * xf, axis=-1, keepdims=True)
    return xf * lax.rsqrt(ms + NORM_EPS) * gain


def _silu(g):
    return g * (1.0 / (1.0 + jnp.exp(-g)))


def _rope(x, cos, sin_signed):
    w = x.shape[-1]
    lane = lax.broadcasted_iota(jnp.int32, x.shape, 1)
    first_half = (lane & (ROPE_SEG - 1)) < ROPE_HALF
    partner = jnp.where(first_half, pltpu.roll(x, w - ROPE_HALF, 1), pltpu.roll(x, ROPE_HALF, 1))
    return x * cos + partner * sin_signed


def _dot(a, b):
    return jnp.dot(a, b, preferred_element_type=F32)


def _dot_nt(a, b):
    return lax.dot_general(a, b, (((1,), (1,)), ((), ())), preferred_element_type=F32)


def _flash_init(tq):
    return jnp.full((1, tq), NEG_INF, F32), jnp.zeros((V_ROWS, tq), F32)


def _flash_step(carry, st, vt):
    m, acc = carry
    m_new = jnp.maximum(m, jnp.max(st, axis=0, keepdims=True))
    alpha = jnp.exp2(m - m_new)
    p = jnp.exp2(st - m_new)
    acc = alpha * acc + _dot(vt, p.astype(BF16))
    return m_new, acc


def _flash_run(n_streams, score_fn, vt_fn, n_steps=N_CHUNKS):
    carries = [_flash_init(TQ) for _ in range(n_streams)]
    st = [score_fn(0, j) for j in range(n_streams)]
    for step in range(n_steps):
        st_next = [score_fn(step + 1, j) for j in range(n_streams)] if step + 1 < n_steps else None
        for j in range(n_streams):
            carries[j] = _flash_step(carries[j], st[j], vt_fn(step, j))
        st = st_next
    return carries


def _flash_finish(carry):
    _, acc = carry
    l = acc[HEAD_W:HEAD_W + 1, :]
    return (acc[:HEAD_W, :] * (1.0 / l)).T


def _store_vt(vt_ref, vt, n_heads):
    ones = jnp.ones((V_ROWS - HEAD_W, vt.shape[1]), BF16)
    for h in range(n_heads):
        vt_ref[h * V_ROWS:h * V_ROWS + HEAD_W, :] = vt[h * HEAD_W:(h + 1) * HEAD_W, :].astype(BF16)
        vt_ref[h * V_ROWS + HEAD_W:(h + 1) * V_ROWS, :] = ones


_E_QK, _E_GA, _E_CQ, _E_CKV, _E_KR, _E_COLS = 0, 3072, 4096, 4864, 5376, 6464
_N_CHUNK = 512


def _even_inproj_kernel(x_ref, pg_ref, w_ref, wvt_ref, qn_ref, kvn_ref, cos_ref, sin_ref,
                        qk_ref, vt_ref, gate_ref, cq_ref, ckv_ref, kpe_ref):
    h = _rms(x_ref[...], pg_ref[...]).astype(BF16)

    def proj(c0, n):
        return _dot(h, w_ref[:, c0:c0 + n])

    q_scale = (A_DQK ** -0.5) * LOG2E
    for c in range(0, 1024, _N_CHUNK):
        qk_ref[:, c:c + _N_CHUNK] = (proj(_E_QK + c, _N_CHUNK) * q_scale).astype(BF16)
    for c in range(1024, 2048, _N_CHUNK):
        qk_ref[:, c:c + _N_CHUNK] = proj(_E_QK + c, _N_CHUNK).astype(BF16)
    _store_vt(vt_ref, _dot_nt(wvt_ref[...], h), HEADS)
    for c in range(0, 1024, _N_CHUNK):
        gate_ref[:, c:c + _N_CHUNK] = _silu(proj(_E_GA + c, _N_CHUNK))
    cq_ref[...] = _rms(proj(_E_CQ, B_Q_RANK), qn_ref[...]).astype(BF16)
    ckv_ref[...] = _rms(proj(_E_CKV, B_KV_RANK), kvn_ref[...]).astype(BF16)
    tail = proj(_E_KR, _E_COLS - _E_KR)
    gate_ref[:, 1024:2048] = _silu(tail[:, B_ROPE:])
    kr = tail[:, :128]
    lane = lax.broadcasted_iota(jnp.int32, kr.shape, 1)
    kpe = jnp.where(lane < B_ROPE, _rope(kr, cos_ref[...], sin_ref[...]), 0.0)
    kpe_ref[...] = kpe.astype(BF16)


def _vt_out_spec(n_heads, tm):
    per_chunk = TK // tm
    return pl.BlockSpec((None, n_heads * V_ROWS, tm), lambda i: (i // per_chunk, 0, i % per_chunk))


def _vt_shape(m, n_heads):
    return jax.ShapeDtypeStruct((m // TK, n_heads * V_ROWS, TK), BF16)


def _even_inproj(x2, pre_gain, w_perm, wvt, q_norm, kv_norm, cos_k, sin_k):
    m = x2.shape[0]
    tm = TM_PROJ
    pos_blocks = SEQ // tm
    row = lambda i: (i, 0)
    fixed = lambda i: (0, 0)
    pos = lambda i: (i % pos_blocks, 0)
    return pl.pallas_call(
        _even_inproj_kernel,
        grid=(m // tm,),
        in_specs=[
            pl.BlockSpec((tm, D_MODEL), row),
            pl.BlockSpec((1, D_MODEL), fixed),
            pl.BlockSpec((D_MODEL, _E_COLS), fixed, pipeline_mode=pl.Buffered(1)),
            pl.BlockSpec((1024, D_MODEL), fixed, pipeline_mode=pl.Buffered(1)),
            pl.BlockSpec((1, B_Q_RANK), fixed),
            pl.BlockSpec((1, B_KV_RANK), fixed),
            pl.BlockSpec((tm, 128), pos),
            pl.BlockSpec((tm, 128), pos),
        ],
        out_specs=[
            pl.BlockSpec((tm, 2048), row),
            _vt_out_spec(HEADS, tm),
            pl.BlockSpec((tm, 2048), row),
            pl.BlockSpec((tm, B_Q_RANK), row),
            pl.BlockSpec((tm, B_KV_RANK), row),
            pl.BlockSpec((tm, 128), row),
        ],
        out_shape=[
            jax.ShapeDtypeStruct((m, 2048), BF16),
            _vt_shape(m, HEADS),
            jax.ShapeDtypeStruct((m, 2048), F32),
            jax.ShapeDtypeStruct((m, B_Q_RANK), BF16),
            jax.ShapeDtypeStruct((m, B_KV_RANK), BF16),
            jax.ShapeDtypeStruct((m, 128), BF16),
        ],
        compiler_params=pltpu.CompilerParams(
            dimension_semantics=("arbitrary",), vmem_limit_bytes=VMEM_LIMIT),
        name="even_inproj",
    )(x2, pre_gain, w_perm, wvt, q_norm, kv_norm, cos_k, sin_k)


def _log2e_pieces():
    pieces, rest = [], np.float64(LOG2E)
    for _ in range(3):
        p = np.float64(np.asarray(rest).astype(BF16))
        pieces.append(p)
        rest = rest - p
    return pieces


def _alibi_aug_tables():
    pos = np.arange(SEQ)
    hi, lo = (pos // 64) * 64.0, (pos % 64) * 1.0
    c = _log2e_pieces()
    qaug = np.zeros((SEQ, HEAD_W))
    kaug = np.zeros((SEQ, HEAD_W))
    for p in range(3):
        qaug[:, p], kaug[:, p] = c[p], hi
        qaug[:, 3 + p], kaug[:, 3 + p] = c[p], lo
        qaug[:, 6 + p], kaug[:, 6 + p] = -hi, c[p]
        qaug[:, 9 + p], kaug[:, 9 + p] = -lo, c[p]
    return jnp.asarray(qaug, F32), jnp.asarray(kaug, BF16)


def _diff_attn_kernel(slopes_ref, q_ref, k_ref, vt_ref, g_ref, qaug_ref, kaug_ref,
                      lq1_ref, lk1_ref, lq2_ref, lk2_ref, subln_ref, o_ref, kcat_ref, *, lambda_init):
    head = pl.program_id(1)
    qi = pl.program_id(2)
    slope = slopes_ref[head]

    @pl.when(qi == 0)
    def _():
        k = k_ref[...]
        kaug = kaug_ref[...]
        kcat_ref[0, :, :HEAD_W] = k
        kcat_ref[0, :, HEAD_W:] = kaug
        kcat_ref[1, :, :HEAD_W] = k
        kcat_ref[1, :, HEAD_W:] = -kaug

    q = q_ref[...]
    lane = lax.broadcasted_iota(jnp.int32, q.shape, 1)
    zero = jnp.zeros_like(q)
    q_maps = (jnp.where(lane < A_DQK, q, zero), jnp.where(lane >= A_DQK, q, zero))
    qaug = (qaug_ref[...] * slope).astype(BF16)
    q_diag = [jnp.concatenate([qm, zero], axis=1) for qm in q_maps]
    q_off = [jnp.concatenate([qm, qaug], axis=1) for qm in q_maps]

    dist = jnp.abs(lax.broadcasted_iota(jnp.int32, (TK, TQ), 0)
                   - lax.broadcasted_iota(jnp.int32, (TK, TQ), 1)).astype(F32)
    diag_bias = dist * (-slope * LOG2E)

    def chunk_of(r):
        wrapped = (qi + r >= N_CHUNKS).astype(jnp.int32)
        return qi + r - wrapped * N_CHUNKS, wrapped

    def score(r, mi):
        c, wrapped = chunk_of(r)
        kc = kcat_ref[1 - wrapped, pl.ds(pl.multiple_of(c * TK, TK), TK), :]
        if r == 0:
            return _dot_nt(kc, q_diag[mi]) + diag_bias
        return _dot_nt(kc, q_off[mi])

    def values(r, mi):
        return vt_ref[chunk_of(r)[0]]

    carries = _flash_run(2, score, values)

    lam = (jnp.exp(jnp.sum(lq1_ref[...] * lk1_ref[...], axis=-1, keepdims=True))
           - jnp.exp(jnp.sum(lq2_ref[...] * lk2_ref[...], axis=-1, keepdims=True))
           + lambda_init)
    o = _flash_finish(carries[0]) - lam * _flash_finish(carries[1])
    o = _rms(o, subln_ref[...]) * (1.0 - lambda_init)
    o_ref[...] = (o * g_ref[...]).astype(BF16)


def _diff_attn(slopes, qk, vt, gate, qaug, kaug, lq1, lk1, lq2, lk2, subln, lambda_init):
    b, s, _ = qk.shape
    vec = lambda bb, h, qi: (0, 0)
    return pl.pallas_call(
        functools.partial(_diff_attn_kernel, lambda_init=lambda_init),
        grid=(b, HEADS, s // TQ),
        in_specs=[
            pl.BlockSpec(memory_space=pltpu.SMEM),
            pl.BlockSpec((None, TQ, HEAD_W), lambda bb, h, qi: (bb, qi, h)),
            pl.BlockSpec((None, s, HEAD_W), lambda bb, h, qi: (bb, 0, HEADS + h)),
            pl.BlockSpec((N_CHUNKS, V_ROWS, TK), lambda bb, h, qi: (bb, h, 0)),
            pl.BlockSpec((None, TQ, HEAD_W), lambda bb, h, qi: (bb, qi, h)),
            pl.BlockSpec((TQ, HEAD_W), lambda bb, h, qi: (qi, 0)),
            pl.BlockSpec((s, HEAD_W), vec),
            pl.BlockSpec((1, A_DQK), vec),
            pl.BlockSpec((1, A_DQK), vec),
            pl.BlockSpec((1, A_DQK), vec),
            pl.BlockSpec((1, A_DQK), vec),
            pl.BlockSpec((1, HEAD_W), vec),
        ],
        out_specs=pl.BlockSpec((None, TQ, HEAD_W), lambda bb, h, qi: (bb, qi, h)),
        out_shape=jax.ShapeDtypeStruct((b, s, HEADS * HEAD_W), BF16),
        scratch_shapes=[pltpu.VMEM((2, s, 2 * HEAD_W), BF16)],
        compiler_params=pltpu.CompilerParams(
            dimension_semantics=("arbitrary",) * 3, vmem_limit_bytes=VMEM_LIMIT),
        name="diff_attn",
    )(slopes, qk, qk, vt, gate, qaug, kaug, lq1, lk1, lq2, lk2, subln)


def _mla_up_kernel(cq_ref, ckv_ref, kpe_ref, wq_ref, wk_ref, wvt_ref, cos_ref, sin_ref,
                   q_ref, k_ref, vt_ref):
    cq = cq_ref[...]
    ckv = ckv_ref[...]
    kpe = kpe_ref[...]
    cos = cos_ref[...]
    sin = sin_ref[...]
    scale = (B_NOPE + B_ROPE) ** -0.5 * LOG2E
    for h in range(HEADS):
        qh = _dot(cq, wq_ref[:, h * B_QK_PAD:(h + 1) * B_QK_PAD])
        q_ref[:, h * B_QK_PAD:(h + 1) * B_QK_PAD] = (_rope(qh, cos, sin) * scale).astype(BF16)
        kh = _dot(ckv, wk_ref[:, h * B_NOPE:(h + 1) * B_NOPE])
        k_ref[:, h * B_QK_PAD:h * B_QK_PAD + B_NOPE] = kh.astype(BF16)
        k_ref[:, h * B_QK_PAD + B_NOPE:(h + 1) * B_QK_PAD] = kpe
    _store_vt(vt_ref, _dot_nt(wvt_ref[...], ckv), HEADS)


def _mla_up(cq, ckv, kpe, wq, wk, wvt, cos_q, sin_q):
    m = cq.shape[0]
    tm = TK
    pos_blocks = SEQ // tm
    row = lambda i: (i, 0)
    fixed = lambda i: (0, 0)
    pos = lambda i: (i % pos_blocks, 0)
    return pl.pallas_call(
        _mla_up_kernel,
        grid=(m // tm,),
        in_specs=[
            pl.BlockSpec((tm, B_Q_RANK), row),
            pl.BlockSpec((tm, B_KV_RANK), row),
            pl.BlockSpec((tm, 128), row),
            pl.BlockSpec(wq.shape, fixed),
            pl.BlockSpec(wk.shape, fixed),
            pl.BlockSpec(wvt.shape, fixed),
            pl.BlockSpec((tm, B_QK_PAD), pos),
            pl.BlockSpec((tm, B_QK_PAD), pos),
        ],
        out_specs=[
            pl.BlockSpec((tm, HEADS * B_QK_PAD), row),
            pl.BlockSpec((tm, HEADS * B_QK_PAD), row),
            _vt_out_spec(HEADS, tm),
        ],
        out_shape=[
            jax.ShapeDtypeStruct((m, HEADS * B_QK_PAD), BF16),
            jax.ShapeDtypeStruct((m, HEADS * B_QK_PAD), BF16),
            _vt_shape(m, HEADS),
        ],
        compiler_params=pltpu.CompilerParams(
            dimension_semantics=("arbitrary",), vmem_limit_bytes=VMEM_LIMIT),
        name="mla_up",
    )(cq, ckv, kpe, wq, wk, wvt, cos_q, sin_q)


def _gated_attn_kernel(q_ref, k_ref, vt_ref, g_ref, o_ref, *, d_qk, heads_per_step, kv_per_step):
    qts = [q_ref[:, j * d_qk:(j + 1) * d_qk].T for j in range(heads_per_step)]
    kv_of = lambda j: j if kv_per_step > 1 else 0

    tk = TK_PLAIN
    per = tk // TK

    def score(c, j):
        jk = kv_of(j)
        return _dot(k_ref[c * tk:(c + 1) * tk, jk * d_qk:(jk + 1) * d_qk], qts[j])

    def values(c, j):
        jk = kv_of(j)
        parts = [vt_ref[c * per + i, jk * V_ROWS:(jk + 1) * V_ROWS, :] for i in range(per)]
        return parts[0] if per == 1 else jnp.concatenate(parts, axis=1)

    carries = _flash_run(heads_per_step, score, values, SEQ // tk)
    for j in range(heads_per_step):
        o = _flash_finish(carries[j]) * g_ref[:, j * HEAD_W:(j + 1) * HEAD_W]
        o_ref[:, j * HEAD_W:(j + 1) * HEAD_W] = o.astype(BF16)


def _gated_attn(q, k, vt, gate, *, d_qk, q_heads_per_kv, heads_per_step, gate_block0, name):
    b, s, _ = q.shape
    hs = heads_per_step
    kv_per_step = max(1, hs // q_heads_per_kv)
    steps_per_kv = max(1, q_heads_per_kv // hs)
    gate0 = gate_block0 // hs
    return pl.pallas_call(
        functools.partial(_gated_attn_kernel, d_qk=d_qk, heads_per_step=hs, kv_per_step=kv_per_step),
        grid=(b, HEADS // hs, s // TQ),
        in_specs=[
            pl.BlockSpec((None, TQ, hs * d_qk), lambda bb, h, qi: (bb, qi, h)),
            pl.BlockSpec((None, s, kv_per_step * d_qk), lambda bb, h, qi: (bb, 0, h // steps_per_kv)),
            pl.BlockSpec((N_CHUNKS, kv_per_step * V_ROWS, TK), lambda bb, h, qi: (bb, h // steps_per_kv, 0)),
            pl.BlockSpec((None, TQ, hs * HEAD_W), lambda bb, h, qi: (bb, qi, gate0 + h)),
        ],
        out_specs=pl.BlockSpec((None, TQ, hs * HEAD_W), lambda bb, h, qi: (bb, qi, h)),
        out_shape=jax.ShapeDtypeStruct((b, s, HEADS * HEAD_W), BF16),
        compiler_params=pltpu.CompilerParams(
            dimension_semantics=("arbitrary",) * 3, vmem_limit_bytes=VMEM_LIMIT),
        name=name,
    )(q, k, vt, gate)


def _outproj_kernel(oa_ref, ob_ref, w_ref, x_ref, pg_ref, y_ref):
    half = oa_ref.shape[-1]
    m = _dot(oa_ref[...], w_ref[:half, :]) + _dot(ob_ref[...], w_ref[half:, :])
    y_ref[...] = x_ref[...] + _rms(m, pg_ref[...])


def _outproj(oa, ob, w, x2, post_gain, name):
    m = x2.shape[0]
    tm = TM_PROJ
    row = lambda i: (i, 0)
    fixed = lambda i: (0, 0)
    return pl.pallas_call(
        _outproj_kernel,
        grid=(m // tm,),
        in_specs=[
            pl.BlockSpec((tm, oa.shape[-1]), row),
            pl.BlockSpec((tm, ob.shape[-1]), row),
            pl.BlockSpec(w.shape, fixed),
            pl.BlockSpec((tm, D_MODEL), row),
            pl.BlockSpec((1, D_MODEL), fixed),
        ],
        out_specs=pl.BlockSpec((tm, D_MODEL), row),
        out_shape=jax.ShapeDtypeStruct((m, D_MODEL), F32),
        compiler_params=pltpu.CompilerParams(
            dimension_semantics=("arbitrary",), vmem_limit_bytes=VMEM_LIMIT),
        name=name,
    )(oa, ob, w, x2, post_gain)


_O_QKV, _O_GC, _O_QD, _O_KD, _O_GD, _O_COLS = 0, 3072, 4096, 5120, 5632, 6656


def _odd_inproj_kernel(x_ref, pg_ref, w_ref, wvt_ref, qn_ref, kn_ref, cos_ref, sin_ref,
                       qkv_ref, vt_ref, gate_ref, qd_ref, kd_ref):
    h = _rms(x_ref[...], pg_ref[...]).astype(BF16)
    cos = cos_ref[...]
    sin = sin_ref[...]
    scale = HEAD_W ** -0.5 * LOG2E

    def proj(c0, n):
        return _dot(h, w_ref[:, c0:c0 + n])

    for c in range(0, 1024, _N_CHUNK):
        qkv_ref[:, c:c + _N_CHUNK] = (proj(_O_QKV + c, _N_CHUNK) * scale).astype(BF16)
    for c in range(1024, 3072, _N_CHUNK):
        qkv_ref[:, c:c + _N_CHUNK] = proj(_O_QKV + c, _N_CHUNK).astype(BF16)
    _store_vt(vt_ref, _dot_nt(wvt_ref[...], h), D_KV_HEADS)
    for c in range(0, 1024, _N_CHUNK):
        gate_ref[:, c:c + _N_CHUNK] = _silu(proj(_O_GC + c, _N_CHUNK))
        gate_ref[:, 1024 + c:1024 + c + _N_CHUNK] = _silu(proj(_O_GD + c, _N_CHUNK))
    for c in range(0, 1024, _N_CHUNK):
        qd = proj(_O_QD + c, _N_CHUNK)
        for j in range(0, _N_CHUNK, HEAD_W):
            qh = _rope(_rms(qd[:, j:j + HEAD_W], qn_ref[...]), cos, sin) * scale
            qd_ref[:, c + j:c + j + HEAD_W] = qh.astype(BF16)
    kd = proj(_O_KD, 256)
    for j in range(0, 256, HEAD_W):
        kh = _rope(_rms(kd[:, j:j + HEAD_W], kn_ref[...]), cos, sin)
        kd_ref[:, j:j + HEAD_W] = kh.astype(BF16)


def _odd_inproj(x2, pre_gain, w_perm, wvt, q_norm, k_norm, cos_ax, sin_ax):
    m = x2.shape[0]
    tm = TM_PROJ
    pos_blocks = SEQ // tm
    row = lambda i: (i, 0)
    fixed = lambda i: (0, 0)
    pos = lambda i: (i % pos_blocks, 0)
    return pl.pallas_call(
        _odd_inproj_kernel,
        grid=(m // tm,),
        in_specs=[
            pl.BlockSpec((tm, D_MODEL), row),
            pl.BlockSpec((1, D_MODEL), fixed),
            pl.BlockSpec((D_MODEL, _O_COLS), fixed, pipeline_mode=pl.Buffered(1)),
            pl.BlockSpec((256, D_MODEL), fixed, pipeline_mode=pl.Buffered(1)),
            pl.BlockSpec((1, HEAD_W), fixed),
            pl.BlockSpec((1, HEAD_W), fixed),
            pl.BlockSpec((tm, HEAD_W), pos),
            pl.BlockSpec((tm, HEAD_W), pos),
        ],
        out_specs=[
            pl.BlockSpec((tm, 3072), row),
            _vt_out_spec(D_KV_HEADS, tm),
            pl.BlockSpec((tm, 2048), row),
            pl.BlockSpec((tm, 1024), row),
            pl.BlockSpec((tm, 256), row),
        ],
        out_shape=[
            jax.ShapeDtypeStruct((m, 3072), BF16),
            _vt_shape(m, D_KV_HEADS),
            jax.ShapeDtypeStruct((m, 2048), F32),
            jax.ShapeDtypeStruct((m, 1024), BF16),
            jax.ShapeDtypeStruct((m, 256), BF16),
        ],
        compiler_params=pltpu.CompilerParams(
            dimension_semantics=("arbitrary",), vmem_limit_bytes=VMEM_LIMIT),
        name="odd_inproj",
    )(x2, pre_gain, w_perm, wvt, q_norm, k_norm, cos_ax, sin_ax)


_NA_Q = NA_GROUP_ROWS * GRID_W
_NA_K = NA_BAND_ROWS * GRID_W
_NA_GROUPS = SEQ // _NA_Q
_NA_ROWS = SEQ // GRID_W
_NA_LAST_START = _NA_ROWS - NA_BAND_ROWS
_NA_CLASS_GROUPS = (0, 1, _NA_GROUPS - 1)


def _na_band_start(g):
    lo = g * NA_GROUP_ROWS - NA_KH // 2
    if isinstance(g, int):
        return min(max(lo, 0), _NA_LAST_START)
    return jnp.clip(lo, 0, _NA_LAST_START)


def _na_build_bias(rpb_ref, bias_ref):
    vec = rpb_ref[...] * LOG2E
    even_src = pltpu.roll(vec, 128 - (NA_KW - 1), 1)
    odd_src = pltpu.roll(vec, GRID_W - (NA_KW - 1), 1)
    q_idx = lax.broadcasted_iota(jnp.int32, (GRID_W, 128), 0)
    lane = lax.broadcasted_iota(jnp.int32, (GRID_W, 128), 1)
    w_idx = lane & (GRID_W - 1)
    c0 = jnp.clip(q_idx - NA_KW // 2, 0, GRID_W - NA_KW)
    col_in = (w_idx >= c0) & (w_idx < c0 + NA_KW)
    neg = jnp.full((GRID_W, 128), NEG_INF, F32)
    cache = {}
    for cls, g in enumerate(_NA_CLASS_GROUPS):
        start = _na_band_start(g)
        for qr in range(NA_GROUP_ROWS):
            rq = g * NA_GROUP_ROWS + qr
            r0 = min(max(rq - NA_KH // 2, 0), _NA_ROWS - NA_KH)
            for kp in range(NA_BAND_ROWS // 2):
                halves = []
                for rk in (start + 2 * kp, start + 2 * kp + 1):
                    halves.append(rk - rq + NA_KH - 1 if r0 <= rk < r0 + NA_KH else None)
                key = tuple(halves)
                if key not in cache:
                    if halves[0] is None and halves[1] is None:
                        cache[key] = neg
                    else:
                        src = jnp.zeros((1, 128), F32)
                        mask = None
                        if halves[0] is not None:
                            src = src + even_src[halves[0]:halves[0] + 1, :]
                            mask = lane < GRID_W
                        if halves[1] is not None:
                            src = src + odd_src[halves[1]:halves[1] + 1, :]
                            mask = (lane >= GRID_W) if mask is None else None
                        tile = pltpu.roll(jnp.broadcast_to(src, (GRID_W, 128)), 0, 1, stride=1, stride_axis=0)
                        keep = col_in if mask is None else (col_in & mask)
                        cache[key] = jnp.where(keep, tile, neg)
                bias_ref[cls, qr * GRID_W:(qr + 1) * GRID_W, kp * 128:(kp + 1) * 128] = cache[key]


def _na_kernel(q_ref, k_ref, v_ref, rpb_ref, g_ref, o_ref, bias_ref):
    g = pl.program_id(2)

    @pl.when(g == 0)
    def _():
        for j in range(NA_HEADS_PER_STEP):
            _na_build_bias(rpb_ref.at[j], bias_ref.at[j])

    cls = jnp.where(g == 0, 0, jnp.where(g == _NA_GROUPS - 1, 2, 1))
    start = pl.multiple_of(_na_band_start(g) * GRID_W, GRID_W)
    heads = [slice(j * HEAD_W, (j + 1) * HEAD_W) for j in range(NA_HEADS_PER_STEP)]
    scores = [_dot_nt(q_ref[:, hd], k_ref[pl.ds(start, _NA_K), hd]) + bias_ref[j, cls]
              for j, hd in enumerate(heads)]
    for hd, s in zip(heads, scores):
        m = jnp.max(s, axis=1, keepdims=True)
        p = jnp.exp2(s - m)
        l = jnp.sum(p, axis=1, keepdims=True)
        o = _dot(p.astype(BF16), v_ref[pl.ds(start, _NA_K), hd]) * (1.0 / l)
        o_ref[:, hd] = (o * g_ref[:, hd]).astype(BF16)


def _na_attn(qkv, gate, rpb_pad):
    b, s, _ = qkv.shape
    hs = NA_HEADS_PER_STEP
    steps = HEADS // hs
    return pl.pallas_call(
        _na_kernel,
        grid=(b, steps, _NA_GROUPS),
        in_specs=[
            pl.BlockSpec((None, _NA_Q, hs * HEAD_W), lambda bb, h, g: (bb, g, h)),
            pl.BlockSpec((None, s, hs * HEAD_W), lambda bb, h, g: (bb, 0, steps + h)),
            pl.BlockSpec((None, s, hs * HEAD_W), lambda bb, h, g: (bb, 0, 2 * steps + h)),
            pl.BlockSpec((hs, 16, 128), lambda bb, h, g: (h, 0, 0)),
            pl.BlockSpec((None, _NA_Q, hs * HEAD_W), lambda bb, h, g: (bb, g, h)),
        ],
        out_specs=pl.BlockSpec((None, _NA_Q, hs * HEAD_W), lambda bb, h, g: (bb, g, h)),
        out_shape=jax.ShapeDtypeStruct((b, s, HEADS * HEAD_W), BF16),
        scratch_shapes=[pltpu.VMEM((hs, 3, _NA_Q, _NA_K), F32)],
        compiler_params=pltpu.CompilerParams(
            dimension_semantics=("arbitrary",) * 3, vmem_limit_bytes=VMEM_LIMIT),
        name="na_attn",
    )(qkv, qkv, qkv, rpb_pad, gate)


def _rope_seg_tables(pos):
    inv = ROPE_THETA ** (-np.arange(ROPE_HALF, dtype=np.float64) / ROPE_HALF)
    ang = pos.astype(np.float64)[:, None] * inv[None, :]
    cos, sin = np.cos(ang), np.sin(ang)
    return np.concatenate([cos, cos], axis=1), np.concatenate([-sin, sin], axis=1)


def _rope_tables():
    t = np.arange(SEQ)
    cos_t, sin_t = _rope_seg_tables(t)
    ones, zeros = np.ones((SEQ, ROPE_SEG)), np.zeros((SEQ, ROPE_SEG))
    cos_k = np.concatenate([cos_t, ones], axis=1)
    sin_k = np.concatenate([sin_t, zeros], axis=1)
    cos_q = np.concatenate([ones, ones, cos_t, ones], axis=1)
    sin_q = np.concatenate([zeros, zeros, sin_t, zeros], axis=1)
    cos_r, sin_r = _rope_seg_tables(t // GRID_W)
    cos_c, sin_c = _rope_seg_tables(t % GRID_W)
    cos_ax = np.concatenate([cos_r, cos_c], axis=1)
    sin_ax = np.concatenate([sin_r, sin_c], axis=1)
    as_f32 = lambda a: jnp.asarray(a, F32)
    return tuple(map(as_f32, (cos_k, sin_k, cos_q, sin_q, cos_ax, sin_ax)))


def _transpose_cols_kernel(w_ref, o_ref):
    o_ref[...] = w_ref[...].T.astype(BF16)


def _transpose_cols(w, col0, width):
    k = w.shape[0]
    rows = 256
    assert col0 % width == 0 and k % rows == 0
    return pl.pallas_call(
        _transpose_cols_kernel,
        grid=(k // rows,),
        in_specs=[pl.BlockSpec((rows, width), lambda i: (i, col0 // width))],
        out_specs=pl.BlockSpec((width, rows), lambda i: (0, i)),
        out_shape=jax.ShapeDtypeStruct((width, k), BF16),
        name="transpose_cols",
    )(w)


def _even_weights(w_in, w_uq, w_ukv):
    w_bf = w_in.astype(BF16)
    va_t = _transpose_cols(w_in, 2048, 1024)
    wq = w_uq.reshape(B_Q_RANK, HEADS, B_NOPE + B_ROPE)
    wq = jnp.pad(wq, ((0, 0), (0, 0), (0, B_QK_PAD - B_NOPE - B_ROPE))).reshape(B_Q_RANK, HEADS * B_QK_PAD)
    wkv = w_ukv.reshape(B_KV_RANK, HEADS, B_NOPE + HEAD_W)
    wk = wkv[:, :, :B_NOPE].reshape(B_KV_RANK, HEADS * B_NOPE)
    wv = wkv[:, :, B_NOPE:].reshape(B_KV_RANK, HEADS * HEAD_W)
    return (w_bf, va_t, wq.astype(BF16), wk.astype(BF16), wv.T.astype(BF16))


def _odd_weights(w_in):
    return w_in.astype(BF16), _transpose_cols(w_in, 5376, 256)


def _even_layer(x2, batch, layer, pre_gain, post_gain, w_in, w_out, lq1, lk1, lq2, lk2, subln,
                q_norm, w_uq, kv_norm, w_ukv, tables):
    cos_k, sin_k, cos_q, sin_q, _, _ = tables
    lambda_init = 0.8 - 0.6 * math.exp(-0.3 * layer)
    w_perm, wvat, wq, wk, wvbt = _even_weights(w_in, w_uq, w_ukv)
    qk, vat, gate, cq, ckv, kpe = _even_inproj(
        x2, pre_gain[None], w_perm, wvat, q_norm[None], kv_norm[None], cos_k, sin_k)
    as3 = lambda a: a.reshape(batch, SEQ, a.shape[-1])
    slopes = jnp.asarray(2.0 ** (-8.0 * np.arange(1, HEADS + 1) / HEADS), F32)
    qaug, kaug = _alibi_aug_tables()
    oa = _diff_attn(slopes, as3(qk), vat, as3(gate), qaug, kaug, lq1[None], lk1[None], lq2[None],
                    lk2[None], subln[None], lambda_init)
    q_b, k_b, vbt = _mla_up(cq, ckv, kpe, wq, wk, wvbt, cos_q, sin_q)
    ob = _gated_attn(as3(q_b), as3(k_b), vbt, as3(gate), d_qk=B_QK_PAD, q_heads_per_kv=1,
                     heads_per_step=2, gate_block0=HEADS, name="latent_attn")
    m = x2.shape[0]
    return _outproj(oa.reshape(m, -1), ob.reshape(m, -1), w_out.astype(BF16), x2, post_gain[None],
                    "even_outproj")


def _odd_layer(x2, batch, pre_gain, post_gain, w_in, w_out, rpb, q_norm, k_norm, tables):
    cos_ax, sin_ax = tables[4], tables[5]
    w_perm, wvdt = _odd_weights(w_in)
    qkv, vdt, gate, qd, kd = _odd_inproj(
        x2, pre_gain[None], w_perm, wvdt, q_norm[None], k_norm[None], cos_ax, sin_ax)
    as3 = lambda a: a.reshape(batch, SEQ, a.shape[-1])
    rpb_pad = jnp.pad(rpb, ((0, 0), (0, 16 - rpb.shape[1]), (0, 128 - rpb.shape[2])))
    oc = _na_attn(as3(qkv), as3(gate), rpb_pad)
    od = _gated_attn(as3(qd), as3(kd), vdt, as3(gate), d_qk=HEAD_W,
                     q_heads_per_kv=HEADS // D_KV_HEADS, heads_per_step=2, gate_block0=HEADS,
                     name="gqa_attn")
    m = x2.shape[0]
    return _outproj(oc.reshape(m, -1), od.reshape(m, -1), w_out.astype(BF16), x2, post_gain[None],
                    "odd_outproj")


def kernel(x, pre_norm, post_norm, even_w_in, even_w_out, diff_lambda_q1, diff_lambda_k1,
           diff_lambda_q2, diff_lambda_k2, diff_subln, mla_q_norm, mla_w_uq, mla_kv_norm, mla_w_ukv,
           odd_w_in, odd_w_out, na_rpb, gqa_q_norm, gqa_k_norm):
    batch, seq, d_model = x.shape
    assert (seq, d_model) == (SEQ, D_MODEL)
    depth = pre_norm.shape[0]
    tables = _rope_tables()
    x2 = x.reshape(batch * seq, d_model)
    for layer in range(depth):
        i = layer // 2
        if layer % 2 == 0:
            x2 = _even_layer(x2, batch, layer, pre_norm[layer], post_norm[layer], even_w_in[i],
                             even_w_out[i], diff_lambda_q1[i], diff_lambda_k1[i], diff_lambda_q2[i],
                             diff_lambda_k2[i], diff_subln[i], mla_q_norm[i], mla_w_uq[i],
                             mla_kv_norm[i], mla_w_ukv[i], tables)
        else:
            x2 = _odd_layer(x2, batch, pre_norm[layer], post_norm[layer], odd_w_in[i], odd_w_out[i],
                            na_rpb[i], gqa_q_norm[i], gqa_k_norm[i], tables)
    return x2.reshape(batch, seq, d_model)
```

```python
import functools
import math

import numpy as np
import jax
import jax.numpy as jnp
from jax import lax
from jax.experimental import pallas as pl
from jax.experimental.pallas import tpu as pltpu

F32 = jnp.float32
BF16 = jnp.bfloat16

D_MODEL = 2048
SEQ = 4096
GRID_W = 64
NORM_EPS = 1e-6
ROPE_THETA = 10000.0
NEG_INF = -1e30
LOG2E = math.log2(math.e)

HEADS = 8
HEAD_W = 128
V_ROWS = HEAD_W + 16
A_DQK = 64
B_NOPE = 128
B_ROPE = 64
B_Q_RANK = 768
B_KV_RANK = 512
B_QK_PAD = 256
D_KV_HEADS = 2
NA_KH = 8
NA_KW = 16
NA_GROUP_ROWS = 4
NA_BAND_ROWS = 12
NA_HEADS_PER_STEP = 4

ROPE_SEG = 64
ROPE_HALF = 32

VMEM_LIMIT = 56 * 1024 * 1024

TM_PROJ = 256
TM_OUT = 512
TQ = 512
TK = 512
N_CHUNKS = SEQ // TK
TQ_DIFF = 1024
_DIAG_CHUNKS = TQ_DIFF // TK
TQ_PLAIN = 1024
TK_PLAIN = 512


def _rms(xf, gain):
    ms = jnp.mean(xf * xf, axis=-1, keepdims=True)
    return xf * lax.rsqrt(ms + NORM_EPS) * gain


def _silu(g):
    return g * (1.0 / (1.0 + jnp.exp(-g)))


def _rope(x, cos, sin_signed):
    w = x.shape[-1]
    lane = lax.broadcasted_iota(jnp.int32, x.shape, 1)
    first_half = (lane & (ROPE_SEG - 1)) < ROPE_HALF
    partner = jnp.where(first_half, pltpu.roll(x, w - ROPE_HALF, 1), pltpu.roll(x, ROPE_HALF, 1))
    return x * cos + partner * sin_signed


def _dot(a, b):
    return jnp.dot(a, b, preferred_element_type=F32)


def _dot_nt(a, b):
    return lax.dot_general(a, b, (((1,), (1,)), ((), ())), preferred_element_type=F32)


def _flash_init(tq):
    return jnp.full((1, tq), NEG_INF, F32), jnp.zeros((V_ROWS, tq), F32)


def _flash_step(carry, st, vt):
    m, acc = carry
    m_new = jnp.maximum(m, jnp.max(st, axis=0, keepdims=True))
    alpha = jnp.exp2(m - m_new)
    p = jnp.exp2(st - m_new)
    acc = alpha * acc + _dot(vt, p.astype(BF16))
    return m_new, acc


def _flash_run(n_streams, score_fn, vt_fn, n_steps=N_CHUNKS, tq=TQ):
    carries = [_flash_init(tq) for _ in range(n_streams)]
    st = [score_fn(0, j) for j in range(n_streams)]
    for step in range(n_steps):
        st_next = [score_fn(step + 1, j) for j in range(n_streams)] if step + 1 < n_steps else None
        for j in range(n_streams):
            carries[j] = _flash_step(carries[j], st[j], vt_fn(step, j))
        st = st_next
    return carries


def _flash_finish(carry):
    _, acc = carry
    l = acc[HEAD_W:HEAD_W + 1, :]
    return (acc[:HEAD_W, :] * (1.0 / l)).T


def _store_vt(vt_ref, vt, n_heads):
    ones = jnp.ones((V_ROWS - HEAD_W, vt.shape[1]), BF16)
    for h in range(n_heads):
        vt_ref[h * V_ROWS:h * V_ROWS + HEAD_W, :] = vt[h * HEAD_W:(h + 1) * HEAD_W, :].astype(BF16)
        vt_ref[h * V_ROWS + HEAD_W:(h + 1) * V_ROWS, :] = ones


_E_QK, _E_GA, _E_CQ, _E_CKV, _E_KR, _E_COLS = 0, 3072, 4096, 4864, 5376, 6464
_N_CHUNK = 512


def _even_inproj_kernel(x_ref, pg_ref, w_ref, wvt_ref, qn_ref, kvn_ref, cos_ref, sin_ref,
                        qk_ref, vt_ref, gate_ref, cq_ref, ckv_ref, kpe_ref):
    h = _rms(x_ref[...], pg_ref[...]).astype(BF16)

    def proj(c0, n):
        return _dot(h, w_ref[:, c0:c0 + n])

    q_scale = (A_DQK ** -0.5) * LOG2E
    for c in range(0, 1024, _N_CHUNK):
        qk_ref[:, c:c + _N_CHUNK] = (proj(_E_QK + c, _N_CHUNK) * q_scale).astype(BF16)
    for c in range(1024, 2048, _N_CHUNK):
        qk_ref[:, c:c + _N_CHUNK] = proj(_E_QK + c, _N_CHUNK).astype(BF16)
    _store_vt(vt_ref, _dot_nt(wvt_ref[...], h), HEADS)
    for c in range(0, 1024, _N_CHUNK):
        gate_ref[:, c:c + _N_CHUNK] = _silu(proj(_E_GA + c, _N_CHUNK))
    cq_ref[...] = _rms(proj(_E_CQ, B_Q_RANK), qn_ref[...]).astype(BF16)
    ckv_ref[...] = _rms(proj(_E_CKV, B_KV_RANK), kvn_ref[...]).astype(BF16)
    tail = proj(_E_KR, _E_COLS - _E_KR)
    gate_ref[:, 1024:2048] = _silu(tail[:, B_ROPE:])
    kr = tail[:, :128]
    lane = lax.broadcasted_iota(jnp.int32, kr.shape, 1)
    kpe = jnp.where(lane < B_ROPE, _rope(kr, cos_ref[...], sin_ref[...]), 0.0)
    kpe_ref[...] = kpe.astype(BF16)


def _vt_out_spec(n_heads, tm):
    per_chunk = TK // tm
    return pl.BlockSpec((None, n_heads * V_ROWS, tm), lambda i: (i // per_chunk, 0, i % per_chunk))


def _vt_shape(m, n_heads):
    return jax.ShapeDtypeStruct((m // TK, n_heads * V_ROWS, TK), BF16)


def _even_inproj(x2, pre_gain, w_perm, wvt, q_norm, kv_norm, cos_k, sin_k):
    m = x2.shape[0]
    tm = TM_PROJ
    pos_blocks = SEQ // tm
    row = lambda i: (i, 0)
    fixed = lambda i: (0, 0)
    pos = lambda i: (i % pos_blocks, 0)
    return pl.pallas_call(
        _even_inproj_kernel,
        grid=(m // tm,),
        in_specs=[
            pl.BlockSpec((tm, D_MODEL), row),
            pl.BlockSpec((1, D_MODEL), fixed),
            pl.BlockSpec((D_MODEL, _E_COLS), fixed, pipeline_mode=pl.Buffered(1)),
            pl.BlockSpec((1024, D_MODEL), fixed, pipeline_mode=pl.Buffered(1)),
            pl.BlockSpec((1, B_Q_RANK), fixed),
            pl.BlockSpec((1, B_KV_RANK), fixed),
            pl.BlockSpec((tm, 128), pos),
            pl.BlockSpec((tm, 128), pos),
        ],
        out_specs=[
            pl.BlockSpec((tm, 2048), row),
            _vt_out_spec(HEADS, tm),
            pl.BlockSpec((tm, 2048), row),
            pl.BlockSpec((tm, B_Q_RANK), row),
            pl.BlockSpec((tm, B_KV_RANK), row),
            pl.BlockSpec((tm, 128), row),
        ],
        out_shape=[
            jax.ShapeDtypeStruct((m, 2048), BF16),
            _vt_shape(m, HEADS),
            jax.ShapeDtypeStruct((m, 2048), F32),
            jax.ShapeDtypeStruct((m, B_Q_RANK), BF16),
            jax.ShapeDtypeStruct((m, B_KV_RANK), BF16),
            jax.ShapeDtypeStruct((m, 128), BF16),
        ],
        compiler_params=pltpu.CompilerParams(
            dimension_semantics=("arbitrary",), vmem_limit_bytes=VMEM_LIMIT),
        name="even_inproj",
    )(x2, pre_gain, w_perm, wvt, q_norm, kv_norm, cos_k, sin_k)


def _log2e_pieces():
    pieces, rest = [], np.float64(LOG2E)
    for _ in range(3):
        p = np.float64(np.asarray(rest).astype(BF16))
        pieces.append(p)
        rest = rest - p
    return pieces


def _alibi_aug_tables():
    pos = np.arange(SEQ)
    hi, lo = (pos // 64) * 64.0, (pos % 64) * 1.0
    c = _log2e_pieces()
    qaug = np.zeros((SEQ, HEAD_W))
    kaug = np.zeros((SEQ, HEAD_W))
    for p in range(3):
        qaug[:, p], kaug[:, p] = c[p], hi
        qaug[:, 3 + p], kaug[:, 3 + p] = c[p], lo
        qaug[:, 6 + p], kaug[:, 6 + p] = -hi, c[p]
        qaug[:, 9 + p], kaug[:, 9 + p] = -lo, c[p]
    return jnp.asarray(qaug, F32), jnp.asarray(kaug, BF16)


def _diff_attn_kernel(slopes_ref, q_ref, k_ref, vt_ref, g_ref, qaug_ref, kaug_ref,
                      lq1_ref, lk1_ref, lq2_ref, lk2_ref, subln_ref, o_ref, kcat_ref, dbias_ref,
                      *, lambda_init):
    head = pl.program_id(1)
    qi = pl.program_id(2)
    slope = slopes_ref[head]

    @pl.when(qi == 0)
    def _():
        k = k_ref[...]
        kaug = kaug_ref[...]
        kcat_ref[0, :, :HEAD_W] = k
        kcat_ref[0, :, HEAD_W:] = kaug
        kcat_ref[1, :, :HEAD_W] = k
        kcat_ref[1, :, HEAD_W:] = -kaug
        key = lax.broadcasted_iota(jnp.int32, (TK, TQ_DIFF), 0)
        qry = lax.broadcasted_iota(jnp.int32, (TK, TQ_DIFF), 1)
        for r in range(_DIAG_CHUNKS):
            dbias_ref[r] = jnp.abs(qry - key - r * TK).astype(F32) * (-slope * LOG2E)

    q = q_ref[...]
    lane = lax.broadcasted_iota(jnp.int32, q.shape, 1)
    zero = jnp.zeros_like(q)
    q_maps = (jnp.where(lane < A_DQK, q, zero), jnp.where(lane >= A_DQK, q, zero))
    qaug = (qaug_ref[...] * slope).astype(BF16)
    q_diag = [jnp.concatenate([qm, zero], axis=1) for qm in q_maps]
    q_off = [jnp.concatenate([qm, qaug], axis=1) for qm in q_maps]

    def chunk_of(r):
        first = qi * _DIAG_CHUNKS
        wrapped = (first + r >= N_CHUNKS).astype(jnp.int32)
        return first + r - wrapped * N_CHUNKS, wrapped

    def score(r, mi):
        c, wrapped = chunk_of(r)
        kc = kcat_ref[1 - wrapped, pl.ds(pl.multiple_of(c * TK, TK), TK), :]
        if r < _DIAG_CHUNKS:
            return _dot_nt(kc, q_diag[mi]) + dbias_ref[r]
        return _dot_nt(kc, q_off[mi])

    def values(r, mi):
        return vt_ref[chunk_of(r)[0]]

    carries = _flash_run(2, score, values, N_CHUNKS, TQ_DIFF)

    lam = (jnp.exp(jnp.sum(lq1_ref[...] * lk1_ref[...], axis=-1, keepdims=True))
           - jnp.exp(jnp.sum(lq2_ref[...] * lk2_ref[...], axis=-1, keepdims=True))
           + lambda_init)
    o = _flash_finish(carries[0]) - lam * _flash_finish(carries[1])
    o = _rms(o, subln_ref[...]) * (1.0 - lambda_init)
    o_ref[...] = (o * g_ref[...]).astype(BF16)


def _diff_attn(slopes, qk, vt, gate, qaug, kaug, lq1, lk1, lq2, lk2, subln, lambda_init):
    b, s, _ = qk.shape
    vec = lambda bb, h, qi: (0, 0)
    return pl.pallas_call(
        functools.partial(_diff_attn_kernel, lambda_init=lambda_init),
        grid=(b, HEADS, s // TQ_DIFF),
        in_specs=[
            pl.BlockSpec(memory_space=pltpu.SMEM),
            pl.BlockSpec((None, TQ_DIFF, HEAD_W), lambda bb, h, qi: (bb, qi, h)),
            pl.BlockSpec((None, s, HEAD_W), lambda bb, h, qi: (bb, 0, HEADS + h)),
            pl.BlockSpec((N_CHUNKS, V_ROWS, TK), lambda bb, h, qi: (bb, h, 0)),
            pl.BlockSpec((None, TQ_DIFF, HEAD_W), lambda bb, h, qi: (bb, qi, h)),
            pl.BlockSpec((TQ_DIFF, HEAD_W), lambda bb, h, qi: (qi, 0)),
            pl.BlockSpec((s, HEAD_W), vec),
            pl.BlockSpec((1, A_DQK), vec),
            pl.BlockSpec((1, A_DQK), vec),
            pl.BlockSpec((1, A_DQK), vec),
            pl.BlockSpec((1, A_DQK), vec),
            pl.BlockSpec((1, HEAD_W), vec),
        ],
        out_specs=pl.BlockSpec((None, TQ_DIFF, HEAD_W), lambda bb, h, qi: (bb, qi, h)),
        out_shape=jax.ShapeDtypeStruct((b, s, HEADS * HEAD_W), BF16),
        scratch_shapes=[pltpu.VMEM((2, s, 2 * HEAD_W), BF16),
                        pltpu.VMEM((_DIAG_CHUNKS, TK, TQ_DIFF), F32)],
        compiler_params=pltpu.CompilerParams(
            dimension_semantics=("arbitrary",) * 3, vmem_limit_bytes=VMEM_LIMIT),
        name="diff_attn",
    )(slopes, qk, qk, vt, gate, qaug, kaug, lq1, lk1, lq2, lk2, subln)


def _mla_up_kernel(cq_ref, ckv_ref, kpe_ref, wq_ref, wk_ref, wvt_ref, cos_ref, sin_ref,
                   q_ref, k_ref, vt_ref):
    cq = cq_ref[...]
    ckv = ckv_ref[...]
    kpe = kpe_ref[...]
    cos = cos_ref[...]
    sin = sin_ref[...]
    scale = (B_NOPE + B_ROPE) ** -0.5 * LOG2E
    for h in range(HEADS):
        qh = _dot(cq, wq_ref[:, h * B_QK_PAD:(h + 1) * B_QK_PAD])
        q_ref[:, h * B_QK_PAD:(h + 1) * B_QK_PAD] = (_rope(qh, cos, sin) * scale).astype(BF16)
        k_ref[:, h * B_QK_PAD + B_NOPE:(h + 1) * B_QK_PAD] = kpe
    for h in range(0, HEADS, 2):
        kk = _dot(ckv, wk_ref[:, h * B_NOPE:(h + 2) * B_NOPE]).astype(BF16)
        k_ref[:, h * B_QK_PAD:h * B_QK_PAD + B_NOPE] = kk[:, :B_NOPE]
        k_ref[:, (h + 1) * B_QK_PAD:(h + 1) * B_QK_PAD + B_NOPE] = kk[:, B_NOPE:]
    _store_vt(vt_ref, _dot_nt(wvt_ref[...], ckv), HEADS)


def _mla_up(cq, ckv, kpe, wq, wk, wvt, cos_q, sin_q):
    m = cq.shape[0]
    tm = TK
    pos_blocks = SEQ // tm
    row = lambda i: (i, 0)
    fixed = lambda i: (0, 0)
    pos = lambda i: (i % pos_blocks, 0)
    return pl.pallas_call(
        _mla_up_kernel,
        grid=(m // tm,),
        in_specs=[
            pl.BlockSpec((tm, B_Q_RANK), row),
            pl.BlockSpec((tm, B_KV_RANK), row),
            pl.BlockSpec((tm, 128), row),
            pl.BlockSpec(wq.shape, fixed),
            pl.BlockSpec(wk.shape, fixed),
            pl.BlockSpec(wvt.shape, fixed),
            pl.BlockSpec((tm, B_QK_PAD), pos),
            pl.BlockSpec((tm, B_QK_PAD), pos),
        ],
        out_specs=[
            pl.BlockSpec((tm, HEADS * B_QK_PAD), row),
            pl.BlockSpec((tm, HEADS * B_QK_PAD), row),
            _vt_out_spec(HEADS, tm),
        ],
        out_shape=[
            jax.ShapeDtypeStruct((m, HEADS * B_QK_PAD), BF16),
            jax.ShapeDtypeStruct((m, HEADS * B_QK_PAD), BF16),
            _vt_shape(m, HEADS),
        ],
        compiler_params=pltpu.CompilerParams(
            dimension_semantics=("arbitrary",), vmem_limit_bytes=VMEM_LIMIT),
        name="mla_up",
    )(cq, ckv, kpe, wq, wk, wvt, cos_q, sin_q)


def _gated_attn_kernel(q_ref, k_ref, vt_ref, g_ref, o_ref, *, d_qk, heads_per_step, kv_per_step):
    qts = [q_ref[:, j * d_qk:(j + 1) * d_qk].T for j in range(heads_per_step)]
    kv_of = lambda j: j if kv_per_step > 1 else 0

    tk = TK_PLAIN

    def score(c, j):
        jk = kv_of(j)
        return _dot(k_ref[c * tk:(c + 1) * tk, jk * d_qk:(jk + 1) * d_qk], qts[j])

    def values(c, j):
        rows = slice(kv_of(j) * V_ROWS, (kv_of(j) + 1) * V_ROWS)
        if tk < TK:
            sub = TK // tk
            return vt_ref[c // sub, rows, (c % sub) * tk:(c % sub + 1) * tk]
        parts = [vt_ref[c * (tk // TK) + i, rows, :] for i in range(tk // TK)]
        return parts[0] if len(parts) == 1 else jnp.concatenate(parts, axis=1)

    carries = _flash_run(heads_per_step, score, values, SEQ // tk, TQ_PLAIN)
    for j in range(heads_per_step):
        o = _flash_finish(carries[j]) * g_ref[:, j * HEAD_W:(j + 1) * HEAD_W]
        o_ref[:, j * HEAD_W:(j + 1) * HEAD_W] = o.astype(BF16)


def _gated_attn(q, k, vt, gate, *, d_qk, q_heads_per_kv, heads_per_step, gate_block0, name):
    b, s, _ = q.shape
    hs = heads_per_step
    kv_per_step = max(1, hs // q_heads_per_kv)
    steps_per_kv = max(1, q_heads_per_kv // hs)
    gate0 = gate_block0 // hs
    return pl.pallas_call(
        functools.partial(_gated_attn_kernel, d_qk=d_qk, heads_per_step=hs, kv_per_step=kv_per_step),
        grid=(b, HEADS // hs, s // TQ_PLAIN),
        in_specs=[
            pl.BlockSpec((None, TQ_PLAIN, hs * d_qk), lambda bb, h, qi: (bb, qi, h)),
            pl.BlockSpec((None, s, kv_per_step * d_qk), lambda bb, h, qi: (bb, 0, h // steps_per_kv)),
            pl.BlockSpec((N_CHUNKS, kv_per_step * V_ROWS, TK), lambda bb, h, qi: (bb, h // steps_per_kv, 0)),
            pl.BlockSpec((None, TQ_PLAIN, hs * HEAD_W), lambda bb, h, qi: (bb, qi, gate0 + h)),
        ],
        out_specs=pl.BlockSpec((None, TQ_PLAIN, hs * HEAD_W), lambda bb, h, qi: (bb, qi, h)),
        out_shape=jax.ShapeDtypeStruct((b, s, HEADS * HEAD_W), BF16),
        compiler_params=pltpu.CompilerParams(
            dimension_semantics=("arbitrary",) * 3, vmem_limit_bytes=VMEM_LIMIT),
        name=name,
    )(q, k, vt, gate)


def _outproj_kernel(oa_ref, ob_ref, w_ref, x_ref, pg_ref, y_ref):
    half = oa_ref.shape[-1]
    m = _dot(oa_ref[...], w_ref[:half, :]) + _dot(ob_ref[...], w_ref[half:, :])
    y_ref[...] = x_ref[...] + _rms(m, pg_ref[...])


def _outproj(oa, ob, w, x2, post_gain, name):
    m = x2.shape[0]
    tm = TM_OUT
    row = lambda i: (i, 0)
    fixed = lambda i: (0, 0)
    return pl.pallas_call(
        _outproj_kernel,
        grid=(m // tm,),
        in_specs=[
            pl.BlockSpec((tm, oa.shape[-1]), row),
            pl.BlockSpec((tm, ob.shape[-1]), row),
            pl.BlockSpec(w.shape, fixed, pipeline_mode=pl.Buffered(1)),
            pl.BlockSpec((tm, D_MODEL), row),
            pl.BlockSpec((1, D_MODEL), fixed),
        ],
        out_specs=pl.BlockSpec((tm, D_MODEL), row),
        out_shape=jax.ShapeDtypeStruct((m, D_MODEL), F32),
        compiler_params=pltpu.CompilerParams(
            dimension_semantics=("arbitrary",), vmem_limit_bytes=VMEM_LIMIT),
        name=name,
    )(oa, ob, w, x2, post_gain)


_O_QKV, _O_GC, _O_QD, _O_KD, _O_GD, _O_COLS = 0, 3072, 4096, 5120, 5632, 6656


def _odd_inproj_kernel(x_ref, pg_ref, w_ref, wvt_ref, qn_ref, kn_ref, cos_ref, sin_ref,
                       qkv_ref, vt_ref, gate_ref, qd_ref, kd_ref):
    h = _rms(x_ref[...], pg_ref[...]).astype(BF16)
    cos = cos_ref[...]
    sin = sin_ref[...]
    scale = HEAD_W ** -0.5 * LOG2E

    def proj(c0, n):
        return _dot(h, w_ref[:, c0:c0 + n])

    for c in range(0, 1024, _N_CHUNK):
        qkv_ref[:, c:c + _N_CHUNK] = (proj(_O_QKV + c, _N_CHUNK) * scale).astype(BF16)
    for c in range(1024, 3072, _N_CHUNK):
        qkv_ref[:, c:c + _N_CHUNK] = proj(_O_QKV + c, _N_CHUNK).astype(BF16)
    _store_vt(vt_ref, _dot_nt(wvt_ref[...], h), D_KV_HEADS)
    for c in range(0, 1024, _N_CHUNK):
        gate_ref[:, c:c + _N_CHUNK] = _silu(proj(_O_GC + c, _N_CHUNK))
        gate_ref[:, 1024 + c:1024 + c + _N_CHUNK] = _silu(proj(_O_GD + c, _N_CHUNK))
    for c in range(0, 1024, _N_CHUNK):
        qd = proj(_O_QD + c, _N_CHUNK)
        for j in range(0, _N_CHUNK, HEAD_W):
            qh = _rope(_rms(qd[:, j:j + HEAD_W], qn_ref[...]), cos, sin) * scale
            qd_ref[:, c + j:c + j + HEAD_W] = qh.astype(BF16)
    kd = proj(_O_KD, 256)
    for j in range(0, 256, HEAD_W):
        kh = _rope(_rms(kd[:, j:j + HEAD_W], kn_ref[...]), cos, sin)
        kd_ref[:, j:j + HEAD_W] = kh.astype(BF16)


def _odd_inproj(x2, pre_gain, w_perm, wvt, q_norm, k_norm, cos_ax, sin_ax):
    m = x2.shape[0]
    tm = TM_PROJ
    pos_blocks = SEQ // tm
    row = lambda i: (i, 0)
    fixed = lambda i: (0, 0)
    pos = lambda i: (i % pos_blocks, 0)
    return pl.pallas_call(
        _odd_inproj_kernel,
        grid=(m // tm,),
        in_specs=[
            pl.BlockSpec((tm, D_MODEL), row),
            pl.BlockSpec((1, D_MODEL), fixed),
            pl.BlockSpec((D_MODEL, _O_COLS), fixed, pipeline_mode=pl.Buffered(1)),
            pl.BlockSpec((256, D_MODEL), fixed, pipeline_mode=pl.Buffered(1)),
            pl.BlockSpec((1, HEAD_W), fixed),
            pl.BlockSpec((1, HEAD_W), fixed),
            pl.BlockSpec((tm, HEAD_W), pos),
            pl.BlockSpec((tm, HEAD_W), pos),
        ],
        out_specs=[
            pl.BlockSpec((tm, 3072), row),
            _vt_out_spec(D_KV_HEADS, tm),
            pl.BlockSpec((tm, 2048), row),
            pl.BlockSpec((tm, 1024), row),
            pl.BlockSpec((tm, 256), row),
        ],
        out_shape=[
            jax.ShapeDtypeStruct((m, 3072), BF16),
            _vt_shape(m, D_KV_HEADS),
            jax.ShapeDtypeStruct((m, 2048), F32),
            jax.ShapeDtypeStruct((m, 1024), BF16),
            jax.ShapeDtypeStruct((m, 256), BF16),
        ],
        compiler_params=pltpu.CompilerParams(
            dimension_semantics=("arbitrary",), vmem_limit_bytes=VMEM_LIMIT),
        name="odd_inproj",
    )(x2, pre_gain, w_perm, wvt, q_norm, k_norm, cos_ax, sin_ax)


_NA_Q = NA_GROUP_ROWS * GRID_W
_NA_K = NA_BAND_ROWS * GRID_W
_NA_GROUPS = SEQ // _NA_Q
_NA_ROWS = SEQ // GRID_W
_NA_LAST_START = _NA_ROWS - NA_BAND_ROWS
_NA_CLASS_GROUPS = (0, 1, _NA_GROUPS - 1)


def _na_band_start(g):
    lo = g * NA_GROUP_ROWS - NA_KH // 2
    if isinstance(g, int):
        return min(max(lo, 0), _NA_LAST_START)
    return jnp.clip(lo, 0, _NA_LAST_START)


def _na_build_bias(rpb_ref, bias_ref):
    vec = rpb_ref[...] * LOG2E
    even_src = pltpu.roll(vec, 128 - (NA_KW - 1), 1)
    odd_src = pltpu.roll(vec, GRID_W - (NA_KW - 1), 1)
    q_idx = lax.broadcasted_iota(jnp.int32, (GRID_W, 128), 0)
    lane = lax.broadcasted_iota(jnp.int32, (GRID_W, 128), 1)
    w_idx = lane & (GRID_W - 1)
    c0 = jnp.clip(q_idx - NA_KW // 2, 0, GRID_W - NA_KW)
    col_in = (w_idx >= c0) & (w_idx < c0 + NA_KW)
    neg = jnp.full((GRID_W, 128), NEG_INF, F32)
    cache = {}
    for cls, g in enumerate(_NA_CLASS_GROUPS):
        start = _na_band_start(g)
        for qr in range(NA_GROUP_ROWS):
            rq = g * NA_GROUP_ROWS + qr
            r0 = min(max(rq - NA_KH // 2, 0), _NA_ROWS - NA_KH)
            for kp in range(NA_BAND_ROWS // 2):
                halves = []
                for rk in (start + 2 * kp, start + 2 * kp + 1):
                    halves.append(rk - rq + NA_KH - 1 if r0 <= rk < r0 + NA_KH else None)
                key = tuple(halves)
                if key not in cache:
                    if halves[0] is None and halves[1] is None:
                        cache[key] = neg
                    else:
                        src = jnp.zeros((1, 128), F32)
                        mask = None
                        if halves[0] is not None:
                            src = src + even_src[halves[0]:halves[0] + 1, :]
                            mask = lane < GRID_W
                        if halves[1] is not None:
                            src = src + odd_src[halves[1]:halves[1] + 1, :]
                            mask = (lane >= GRID_W) if mask is None else None
                        tile = pltpu.roll(jnp.broadcast_to(src, (GRID_W, 128)), 0, 1, stride=1, stride_axis=0)
                        keep = col_in if mask is None else (col_in & mask)
                        cache[key] = jnp.where(keep, tile, neg)
                bias_ref[cls, qr * GRID_W:(qr + 1) * GRID_W, kp * 128:(kp + 1) * 128] = cache[key]


def _na_kernel(q_ref, k_ref, v_ref, rpb_ref, g_ref, o_ref, bias_ref):
    g = pl.program_id(2)

    @pl.when(g == 0)
    def _():
        for j in range(NA_HEADS_PER_STEP):
            _na_build_bias(rpb_ref.at[j], bias_ref.at[j])

    cls = jnp.where(g == 0, 0, jnp.where(g == _NA_GROUPS - 1, 2, 1))
    start = pl.multiple_of(_na_band_start(g) * GRID_W, GRID_W)
    heads = [slice(j * HEAD_W, (j + 1) * HEAD_W) for j in range(NA_HEADS_PER_STEP)]
    scores = [_dot_nt(q_ref[:, hd], k_ref[pl.ds(start, _NA_K), hd]) + bias_ref[j, cls]
              for j, hd in enumerate(heads)]
    for hd, s in zip(heads, scores):
        m = jnp.max(s, axis=1, keepdims=True)
        p = jnp.exp2(s - m)
        l = jnp.sum(p, axis=1, keepdims=True)
        o = _dot(p.astype(BF16), v_ref[pl.ds(start, _NA_K), hd]) * (1.0 / l)
        o_ref[:, hd] = (o * g_ref[:, hd]).astype(BF16)


def _na_attn(qkv, gate, rpb_pad):
    b, s, _ = qkv.shape
    hs = NA_HEADS_PER_STEP
    steps = HEADS // hs
    return pl.pallas_call(
        _na_kernel,
        grid=(b, steps, _NA_GROUPS),
        in_specs=[
            pl.BlockSpec((None, _NA_Q, hs * HEAD_W), lambda bb, h, g: (bb, g, h)),
            pl.BlockSpec((None, s, hs * HEAD_W), lambda bb, h, g: (bb, 0, steps + h)),
            pl.BlockSpec((None, s, hs * HEAD_W), lambda bb, h, g: (bb, 0, 2 * steps + h)),
            pl.BlockSpec((hs, 16, 128), lambda bb, h, g: (h, 0, 0)),
            pl.BlockSpec((None, _NA_Q, hs * HEAD_W), lambda bb, h, g: (bb, g, h)),
        ],
        out_specs=pl.BlockSpec((None, _NA_Q, hs * HEAD_W), lambda bb, h, g: (bb, g, h)),
        out_shape=jax.ShapeDtypeStruct((b, s, HEADS * HEAD_W), BF16),
        scratch_shapes=[pltpu.VMEM((hs, 3, _NA_Q, _NA_K), F32)],
        compiler_params=pltpu.CompilerParams(
            dimension_semantics=("arbitrary",) * 3, vmem_limit_bytes=VMEM_LIMIT),
        name="na_attn",
    )(qkv, qkv, qkv, rpb_pad, gate)


def _rope_seg_tables(pos):
    inv = ROPE_THETA ** (-np.arange(ROPE_HALF, dtype=np.float64) / ROPE_HALF)
    ang = pos.astype(np.float64)[:, None] * inv[None, :]
    cos, sin = np.cos(ang), np.sin(ang)
    return np.concatenate([cos, cos], axis=1), np.concatenate([-sin, sin], axis=1)


def _rope_tables():
    t = np.arange(SEQ)
    cos_t, sin_t = _rope_seg_tables(t)
    ones, zeros = np.ones((SEQ, ROPE_SEG)), np.zeros((SEQ, ROPE_SEG))
    cos_k = np.concatenate([cos_t, ones], axis=1)
    sin_k = np.concatenate([sin_t, zeros], axis=1)
    cos_q = np.concatenate([ones, ones, cos_t, ones], axis=1)
    sin_q = np.concatenate([zeros, zeros, sin_t, zeros], axis=1)
    cos_r, sin_r = _rope_seg_tables(t // GRID_W)
    cos_c, sin_c = _rope_seg_tables(t % GRID_W)
    cos_ax = np.concatenate([cos_r, cos_c], axis=1)
    sin_ax = np.concatenate([sin_r, sin_c], axis=1)
    as_f32 = lambda a: jnp.asarray(a, F32)
    return tuple(map(as_f32, (cos_k, sin_k, cos_q, sin_q, cos_ax, sin_ax)))


def _transpose_cols_kernel(w_ref, o_ref):
    o_ref[...] = w_ref[...].T.astype(BF16)


def _transpose_cols(w, col0, width):
    k = w.shape[0]
    rows = 256
    assert col0 % width == 0 and k % rows == 0
    return pl.pallas_call(
        _transpose_cols_kernel,
        grid=(k // rows,),
        in_specs=[pl.BlockSpec((rows, width), lambda i: (i, col0 // width))],
        out_specs=pl.BlockSpec((width, rows), lambda i: (0, i)),
        out_shape=jax.ShapeDtypeStruct((width, k), BF16),
        name="transpose_cols",
    )(w)


def _even_weights(w_in, w_uq, w_ukv):
    w_bf = w_in.astype(BF16)
    va_t = _transpose_cols(w_in, 2048, 1024)
    wq = w_uq.reshape(B_Q_RANK, HEADS, B_NOPE + B_ROPE)
    wq = jnp.pad(wq, ((0, 0), (0, 0), (0, B_QK_PAD - B_NOPE - B_ROPE))).reshape(B_Q_RANK, HEADS * B_QK_PAD)
    wkv = w_ukv.reshape(B_KV_RANK, HEADS, B_NOPE + HEAD_W)
    wk = wkv[:, :, :B_NOPE].reshape(B_KV_RANK, HEADS * B_NOPE)
    wv = wkv[:, :, B_NOPE:].reshape(B_KV_RANK, HEADS * HEAD_W)
    return (w_bf, va_t, wq.astype(BF16), wk.astype(BF16), wv.T.astype(BF16))


def _odd_weights(w_in):
    return w_in.astype(BF16), _transpose_cols(w_in, 5376, 256)


def _even_layer(x2, batch, layer, pre_gain, post_gain, w_in, w_out, lq1, lk1, lq2, lk2, subln,
                q_norm, w_uq, kv_norm, w_ukv, tables):
    cos_k, sin_k, cos_q, sin_q, _, _ = tables
    lambda_init = 0.8 - 0.6 * math.exp(-0.3 * layer)
    w_perm, wvat, wq, wk, wvbt = _even_weights(w_in, w_uq, w_ukv)
    qk, vat, gate, cq, ckv, kpe = _even_inproj(
        x2, pre_gain[None], w_perm, wvat, q_norm[None], kv_norm[None], cos_k, sin_k)
    as3 = lambda a: a.reshape(batch, SEQ, a.shape[-1])
    slopes = jnp.asarray(2.0 ** (-8.0 * np.arange(1, HEADS + 1) / HEADS), F32)
    qaug, kaug = _alibi_aug_tables()
    oa = _diff_attn(slopes, as3(qk), vat, as3(gate), qaug, kaug, lq1[None], lk1[None], lq2[None],
                    lk2[None], subln[None], lambda_init)
    q_b, k_b, vbt = _mla_up(cq, ckv, kpe, wq, wk, wvbt, cos_q, sin_q)
    ob = _gated_attn(as3(q_b), as3(k_b), vbt, as3(gate), d_qk=B_QK_PAD, q_heads_per_kv=1,
                     heads_per_step=2, gate_block0=HEADS, name="latent_attn")
    m = x2.shape[0]
    return _outproj(oa.reshape(m, -1), ob.reshape(m, -1), w_out.astype(BF16), x2, post_gain[None],
                    "even_outproj")


def _odd_layer(x2, batch, pre_gain, post_gain, w_in, w_out, rpb, q_norm, k_norm, tables):
    cos_ax, sin_ax = tables[4], tables[5]
    w_perm, wvdt = _odd_weights(w_in)
    qkv, vdt, gate, qd, kd = _odd_inproj(
        x2, pre_gain[None], w_perm, wvdt, q_norm[None], k_norm[None], cos_ax, sin_ax)
    as3 = lambda a: a.reshape(batch, SEQ, a.shape[-1])
    rpb_pad = jnp.pad(rpb, ((0, 0), (0, 16 - rpb.shape[1]), (0, 128 - rpb.shape[2])))
    oc = _na_attn(as3(qkv), as3(gate), rpb_pad)
    od = _gated_attn(as3(qd), as3(kd), vdt, as3(gate), d_qk=HEAD_W,
                     q_heads_per_kv=HEADS // D_KV_HEADS, heads_per_step=2, gate_block0=HEADS,
                     name="gqa_attn")
    m = x2.shape[0]
    return _outproj(oc.reshape(m, -1), od.reshape(m, -1), w_out.astype(BF16), x2, post_gain[None],
                    "odd_outproj")


def kernel(x, pre_norm, post_norm, even_w_in, even_w_out, diff_lambda_q1, diff_lambda_k1,
           diff_lambda_q2, diff_lambda_k2, diff_subln, mla_q_norm, mla_w_uq, mla_kv_norm, mla_w_ukv,
           odd_w_in, odd_w_out, na_rpb, gqa_q_norm, gqa_k_norm):
    batch, seq, d_model = x.shape
    assert (seq, d_model) == (SEQ, D_MODEL)
    depth = pre_norm.shape[0]
    tables = _rope_tables()
    x2 = x.reshape(batch * seq, d_model)
    for layer in range(depth):
        i = layer // 2
        if layer % 2 == 0:
            x2 = _even_layer(x2, batch, layer, pre_norm[layer], post_norm[layer], even_w_in[i],
                             even_w_out[i], diff_lambda_q1[i], diff_lambda_k1[i], diff_lambda_q2[i],
                             diff_lambda_k2[i], diff_subln[i], mla_q_norm[i], mla_w_uq[i],
                             mla_kv_norm[i], mla_w_ukv[i], tables)
        else:
            x2 = _odd_layer(x2, batch, pre_norm[layer], post_norm[layer], odd_w_in[i], odd_w_out[i],
                            na_rpb[i], gqa_q_norm[i], gqa_k_norm[i], tables)
    return x2.reshape(batch, seq, d_model)
```

```python
import functools
import math

import numpy as np
import jax
import jax.numpy as jnp
from jax import lax
from jax.experimental import pallas as pl
from jax.experimental.pallas import tpu as pltpu

F32 = jnp.float32
BF16 = jnp.bfloat16

D_MODEL = 2048
SEQ = 4096
GRID_W = 64
NORM_EPS = 1e-6
ROPE_THETA = 10000.0
NEG_INF = -1e30
LOG2E = math.log2(math.e)

HEADS = 8
HEAD_W = 128
V_ROWS = HEAD_W + 16
A_DQK = 64
B_NOPE = 128
B_ROPE = 64
B_Q_RANK = 768
B_KV_RANK = 512
B_QK_PAD = 256
D_KV_HEADS = 2
NA_KH = 8
NA_KW = 16
NA_GROUP_ROWS = 4
NA_BAND_ROWS = 12
NA_HEADS_PER_STEP = 4

ROPE_SEG = 64
ROPE_HALF = 32

VMEM_LIMIT = 56 * 1024 * 1024

TM_PROJ = 256
TM_OUT = 512
TK = 512
N_CHUNKS = SEQ // TK
TQ_DIFF = 1024
_DIAG_CHUNKS = TQ_DIFF // TK
TQ_PLAIN = 1024


def _rms(xf, gain):
    ms = jnp.mean(xf * xf, axis=-1, keepdims=True)
    return xf * lax.rsqrt(ms + NORM_EPS) * gain


def _silu(g):
    return g * (1.0 / (1.0 + jnp.exp(-g)))


def _rope(x, cos, sin_signed):
    w = x.shape[-1]
    lane = lax.broadcasted_iota(jnp.int32, x.shape, 1)
    first_half = (lane & (ROPE_SEG - 1)) < ROPE_HALF
    partner = jnp.where(first_half, pltpu.roll(x, w - ROPE_HALF, 1), pltpu.roll(x, ROPE_HALF, 1))
    return x * cos + partner * sin_signed


def _dot(a, b):
    return jnp.dot(a, b, preferred_element_type=F32)


def _dot_nt(a, b):
    return lax.dot_general(a, b, (((1,), (1,)), ((), ())), preferred_element_type=F32)


def _sampled_max(st):
    tiles = [st[i:i + 8, :] for i in range(0, st.shape[0], 32)]
    return jnp.max(functools.reduce(jnp.maximum, tiles), axis=0, keepdims=True)


def _stream_step(carry, st, vt):
    r, acc = carry
    return r, acc + _dot(vt, jnp.exp2(st - r).astype(BF16))


def _stream_softmax(n_streams, score_fn, vt_fn, n_steps, tq, n_init=1):
    ahead = [[score_fn(s, j) for j in range(n_streams)] for s in range(n_init)]
    carries = []
    for j in range(n_streams):
        r = functools.reduce(jnp.maximum, [_sampled_max(ahead[s][j]) for s in range(n_init)])
        carries.append((r, jnp.zeros((V_ROWS, tq), F32)))
    for step in range(n_steps):
        if step + n_init < n_steps:
            ahead.append([score_fn(step + n_init, j) for j in range(n_streams)])
        cur = ahead.pop(0)
        for j in range(n_streams):
            carries[j] = _stream_step(carries[j], cur[j], vt_fn(step, j))
    accs = [c[1] for c in carries]
    finite = [jnp.min(jnp.where(jnp.isfinite(a), 1.0, 0.0)) for a in accs]
    return accs, functools.reduce(jnp.minimum, finite) < 0.5


def _exact_step(carry, st, vt):
    m, acc = carry
    m_new = jnp.maximum(m, jnp.max(st, axis=0, keepdims=True))
    p = jnp.exp2(st - m_new).astype(BF16)
    return m_new, jnp.exp2(m - m_new) * acc + _dot(vt, p)


def _exact_softmax(n_streams, score_fn, vt_fn, n_static, n_steps, tq):
    carries = [(jnp.full((1, tq), NEG_INF, F32), jnp.zeros((V_ROWS, tq), F32)) for _ in range(n_streams)]
    for step in range(n_static):
        carries = [_exact_step(carries[j], score_fn(step, j), vt_fn(step, j)) for j in range(n_streams)]

    def body(step, carries):
        return tuple(_exact_step(carries[j], score_fn(step, j), vt_fn(step, j)) for j in range(n_streams))

    carries = lax.fori_loop(n_static, n_steps, body, tuple(carries))
    return [c[1] for c in carries]


def _normalised(acc):
    l = acc[HEAD_W:HEAD_W + 1, :]
    return (acc[:HEAD_W, :] * (1.0 / l)).T


def _chunk_rows(c, n):
    if isinstance(c, int):
        return slice(c * n, (c + 1) * n)
    return pl.ds(pl.multiple_of(c * n, n), n)


def _store_vt(vt_ref, vt, n_heads):
    ones = jnp.ones((V_ROWS - HEAD_W, vt.shape[1]), BF16)
    for h in range(n_heads):
        vt_ref[h * V_ROWS:h * V_ROWS + HEAD_W, :] = vt[h * HEAD_W:(h + 1) * HEAD_W, :].astype(BF16)
        vt_ref[h * V_ROWS + HEAD_W:(h + 1) * V_ROWS, :] = ones


_E_QK, _E_GA, _E_CQ, _E_CKV, _E_KR, _E_COLS = 0, 3072, 4096, 4864, 5376, 6464
_N_CHUNK = 512


def _even_inproj_kernel(x_ref, pg_ref, w_ref, wvt_ref, qn_ref, kvn_ref, cos_ref, sin_ref,
                        qk_ref, vt_ref, gate_ref, cq_ref, ckv_ref, kpe_ref):
    h = _rms(x_ref[...], pg_ref[...]).astype(BF16)

    def proj(c0, n):
        return _dot(h, w_ref[:, c0:c0 + n])

    q_scale = (A_DQK ** -0.5) * LOG2E
    for c in range(0, 1024, _N_CHUNK):
        qk_ref[:, c:c + _N_CHUNK] = (proj(_E_QK + c, _N_CHUNK) * q_scale).astype(BF16)
    for c in range(1024, 2048, _N_CHUNK):
        qk_ref[:, c:c + _N_CHUNK] = proj(_E_QK + c, _N_CHUNK).astype(BF16)
    _store_vt(vt_ref, _dot_nt(wvt_ref[...], h), HEADS)
    for c in range(0, 1024, _N_CHUNK):
        gate_ref[:, c:c + _N_CHUNK] = _silu(proj(_E_GA + c, _N_CHUNK))
    cq_ref[...] = _rms(proj(_E_CQ, B_Q_RANK), qn_ref[...]).astype(BF16)
    ckv_ref[...] = _rms(proj(_E_CKV, B_KV_RANK), kvn_ref[...]).astype(BF16)
    tail = proj(_E_KR, _E_COLS - _E_KR)
    gate_ref[:, 1024:2048] = _silu(tail[:, B_ROPE:])
    kr = tail[:, :128]
    lane = lax.broadcasted_iota(jnp.int32, kr.shape, 1)
    kpe = jnp.where(lane < B_ROPE, _rope(kr, cos_ref[...], sin_ref[...]), 0.0)
    kpe_ref[...] = kpe.astype(BF16)


def _vt_out_spec(n_heads, tm):
    per_chunk = TK // tm
    return pl.BlockSpec((None, n_heads * V_ROWS, tm), lambda i: (i // per_chunk, 0, i % per_chunk))


def _vt_shape(m, n_heads):
    return jax.ShapeDtypeStruct((m // TK, n_heads * V_ROWS, TK), BF16)


def _even_inproj(x2, pre_gain, w_perm, wvt, q_norm, kv_norm, cos_k, sin_k):
    m = x2.shape[0]
    tm = TM_PROJ
    pos_blocks = SEQ // tm
    row = lambda i: (i, 0)
    fixed = lambda i: (0, 0)
    pos = lambda i: (i % pos_blocks, 0)
    return pl.pallas_call(
        _even_inproj_kernel,
        grid=(m // tm,),
        in_specs=[
            pl.BlockSpec((tm, D_MODEL), row),
            pl.BlockSpec((1, D_MODEL), fixed),
            pl.BlockSpec((D_MODEL, _E_COLS), fixed, pipeline_mode=pl.Buffered(1)),
            pl.BlockSpec((1024, D_MODEL), fixed, pipeline_mode=pl.Buffered(1)),
            pl.BlockSpec((1, B_Q_RANK), fixed),
            pl.BlockSpec((1, B_KV_RANK), fixed),
            pl.BlockSpec((tm, 128), pos),
            pl.BlockSpec((tm, 128), pos),
        ],
        out_specs=[
            pl.BlockSpec((tm, 2048), row),
            _vt_out_spec(HEADS, tm),
            pl.BlockSpec((tm, 2048), row),
            pl.BlockSpec((tm, B_Q_RANK), row),
            pl.BlockSpec((tm, B_KV_RANK), row),
            pl.BlockSpec((tm, 128), row),
        ],
        out_shape=[
            jax.ShapeDtypeStruct((m, 2048), BF16),
            _vt_shape(m, HEADS),
            jax.ShapeDtypeStruct((m, 2048), F32),
            jax.ShapeDtypeStruct((m, B_Q_RANK), BF16),
            jax.ShapeDtypeStruct((m, B_KV_RANK), BF16),
            jax.ShapeDtypeStruct((m, 128), BF16),
        ],
        compiler_params=pltpu.CompilerParams(
            dimension_semantics=("arbitrary",), vmem_limit_bytes=VMEM_LIMIT),
        name="even_inproj",
    )(x2, pre_gain, w_perm, wvt, q_norm, kv_norm, cos_k, sin_k)


def _log2e_pieces():
    pieces, rest = [], np.float64(LOG2E)
    for _ in range(3):
        p = np.float64(np.asarray(rest).astype(BF16))
        pieces.append(p)
        rest = rest - p
    return pieces


def _alibi_aug_tables():
    pos = np.arange(SEQ)
    hi, lo = (pos // 64) * 64.0, (pos % 64) * 1.0
    c = _log2e_pieces()
    qaug = np.zeros((SEQ, HEAD_W))
    kaug = np.zeros((SEQ, HEAD_W))
    for p in range(3):
        qaug[:, p], kaug[:, p] = c[p], hi
        qaug[:, 3 + p], kaug[:, 3 + p] = c[p], lo
        qaug[:, 6 + p], kaug[:, 6 + p] = -hi, c[p]
        qaug[:, 9 + p], kaug[:, 9 + p] = -lo, c[p]
    return jnp.asarray(qaug, F32), jnp.asarray(kaug, BF16)


def _diff_attn_kernel(slopes_ref, q_ref, k_ref, vt_ref, g_ref, qaug_ref, kaug_ref,
                      lq1_ref, lk1_ref, lq2_ref, lk2_ref, subln_ref, o_ref, kcat_ref, dbias_ref,
                      *, lambda_init):
    head = pl.program_id(1)
    qi = pl.program_id(2)
    slope = slopes_ref[head]

    @pl.when(qi == 0)
    def _():
        k = k_ref[...]
        kaug = kaug_ref[...]
        kcat_ref[0, :, :HEAD_W] = k
        kcat_ref[0, :, HEAD_W:] = kaug
        kcat_ref[1, :, :HEAD_W] = k
        kcat_ref[1, :, HEAD_W:] = -kaug
        key = lax.broadcasted_iota(jnp.int32, (TK, TQ_DIFF), 0)
        qry = lax.broadcasted_iota(jnp.int32, (TK, TQ_DIFF), 1)
        for r in range(_DIAG_CHUNKS):
            dbias_ref[r] = jnp.abs(qry - key - r * TK).astype(F32) * (-slope * LOG2E)

    q = q_ref[...]
    lane = lax.broadcasted_iota(jnp.int32, q.shape, 1)
    zero = jnp.zeros_like(q)
    q_maps = (jnp.where(lane < A_DQK, q, zero), jnp.where(lane >= A_DQK, q, zero))
    qaug = (qaug_ref[...] * slope).astype(BF16)
    q_diag = [jnp.concatenate([qm, zero], axis=1) for qm in q_maps]
    q_off = [jnp.concatenate([qm, qaug], axis=1) for qm in q_maps]

    def chunk_of(r):
        first = qi * _DIAG_CHUNKS
        wrapped = (first + r >= N_CHUNKS).astype(jnp.int32)
        return first + r - wrapped * N_CHUNKS, wrapped

    def score(r, mi):
        c, wrapped = chunk_of(r)
        kc = kcat_ref[1 - wrapped, pl.ds(pl.multiple_of(c * TK, TK), TK), :]
        if isinstance(r, int) and r < _DIAG_CHUNKS:
            return _dot_nt(kc, q_diag[mi]) + dbias_ref[r]
        return _dot_nt(kc, q_off[mi])

    def values(r, mi):
        return vt_ref[chunk_of(r)[0]]

    lam = (jnp.exp(jnp.sum(lq1_ref[...] * lk1_ref[...], axis=-1, keepdims=True))
           - jnp.exp(jnp.sum(lq2_ref[...] * lk2_ref[...], axis=-1, keepdims=True))
           + lambda_init)

    def emit(accs):
        o = _normalised(accs[0]) - lam * _normalised(accs[1])
        o = _rms(o, subln_ref[...]) * (1.0 - lambda_init)
        o_ref[...] = (o * g_ref[...]).astype(BF16)

    accs, overflowed = _stream_softmax(2, score, values, N_CHUNKS, TQ_DIFF, n_init=_DIAG_CHUNKS)
    emit(accs)

    @pl.when(overflowed)
    def _():
        emit(_exact_softmax(2, score, values, _DIAG_CHUNKS, N_CHUNKS, TQ_DIFF))


def _diff_attn(slopes, qk, vt, gate, qaug, kaug, lq1, lk1, lq2, lk2, subln, lambda_init):
    b, s, _ = qk.shape
    vec = lambda bb, h, qi: (0, 0)
    return pl.pallas_call(
        functools.partial(_diff_attn_kernel, lambda_init=lambda_init),
        grid=(b, HEADS, s // TQ_DIFF),
        in_specs=[
            pl.BlockSpec(memory_space=pltpu.SMEM),
            pl.BlockSpec((None, TQ_DIFF, HEAD_W), lambda bb, h, qi: (bb, qi, h)),
            pl.BlockSpec((None, s, HEAD_W), lambda bb, h, qi: (bb, 0, HEADS + h)),
            pl.BlockSpec((N_CHUNKS, V_ROWS, TK), lambda bb, h, qi: (bb, h, 0)),
            pl.BlockSpec((None, TQ_DIFF, HEAD_W), lambda bb, h, qi: (bb, qi, h)),
            pl.BlockSpec((TQ_DIFF, HEAD_W), lambda bb, h, qi: (qi, 0)),
            pl.BlockSpec((s, HEAD_W), vec),
            pl.BlockSpec((1, A_DQK), vec),
            pl.BlockSpec((1, A_DQK), vec),
            pl.BlockSpec((1, A_DQK), vec),
            pl.BlockSpec((1, A_DQK), vec),
            pl.BlockSpec((1, HEAD_W), vec),
        ],
        out_specs=pl.BlockSpec((None, TQ_DIFF, HEAD_W), lambda bb, h, qi: (bb, qi, h)),
        out_shape=jax.ShapeDtypeStruct((b, s, HEADS * HEAD_W), BF16),
        scratch_shapes=[pltpu.VMEM((2, s, 2 * HEAD_W), BF16),
                        pltpu.VMEM((_DIAG_CHUNKS, TK, TQ_DIFF), F32)],
        compiler_params=pltpu.CompilerParams(
            dimension_semantics=("arbitrary",) * 3, vmem_limit_bytes=VMEM_LIMIT),
        name="diff_attn",
    )(slopes, qk, qk, vt, gate, qaug, kaug, lq1, lk1, lq2, lk2, subln)


def _mla_up_kernel(cq_ref, ckv_ref, kpe_ref, wq_ref, wk_ref, wvt_ref, cos_ref, sin_ref,
                   q_ref, k_ref, vt_ref):
    cq = cq_ref[...]
    ckv = ckv_ref[...]
    kpe = kpe_ref[...]
    cos = cos_ref[...]
    sin = sin_ref[...]
    scale = (B_NOPE + B_ROPE) ** -0.5 * LOG2E
    for h in range(HEADS):
        qh = _dot(cq, wq_ref[:, h * B_QK_PAD:(h + 1) * B_QK_PAD])
        q_ref[:, h * B_QK_PAD:(h + 1) * B_QK_PAD] = (_rope(qh, cos, sin) * scale).astype(BF16)
        k_ref[:, h * B_QK_PAD + B_NOPE:(h + 1) * B_QK_PAD] = kpe
    for h in range(0, HEADS, 2):
        kk = _dot(ckv, wk_ref[:, h * B_NOPE:(h + 2) * B_NOPE]).astype(BF16)
        k_ref[:, h * B_QK_PAD:h * B_QK_PAD + B_NOPE] = kk[:, :B_NOPE]
        k_ref[:, (h + 1) * B_QK_PAD:(h + 1) * B_QK_PAD + B_NOPE] = kk[:, B_NOPE:]
    _store_vt(vt_ref, _dot_nt(wvt_ref[...], ckv), HEADS)


def _mla_up(cq, ckv, kpe, wq, wk, wvt, cos_q, sin_q):
    m = cq.shape[0]
    tm = TK
    pos_blocks = SEQ // tm
    row = lambda i: (i, 0)
    fixed = lambda i: (0, 0)
    pos = lambda i: (i % pos_blocks, 0)
    return pl.pallas_call(
        _mla_up_kernel,
        grid=(m // tm,),
        in_specs=[
            pl.BlockSpec((tm, B_Q_RANK), row),
            pl.BlockSpec((tm, B_KV_RANK), row),
            pl.BlockSpec((tm, 128), row),
            pl.BlockSpec(wq.shape, fixed),
            pl.BlockSpec(wk.shape, fixed),
            pl.BlockSpec(wvt.shape, fixed),
            pl.BlockSpec((tm, B_QK_PAD), pos),
            pl.BlockSpec((tm, B_QK_PAD), pos),
        ],
        out_specs=[
            pl.BlockSpec((tm, HEADS * B_QK_PAD), row),
            pl.BlockSpec((tm, HEADS * B_QK_PAD), row),
            _vt_out_spec(HEADS, tm),
        ],
        out_shape=[
            jax.ShapeDtypeStruct((m, HEADS * B_QK_PAD), BF16),
            jax.ShapeDtypeStruct((m, HEADS * B_QK_PAD), BF16),
            _vt_shape(m, HEADS),
        ],
        compiler_params=pltpu.CompilerParams(
            dimension_semantics=("arbitrary",), vmem_limit_bytes=VMEM_LIMIT),
        name="mla_up",
    )(cq, ckv, kpe, wq, wk, wvt, cos_q, sin_q)


def _gated_attn_kernel(q_ref, k_ref, vt_ref, g_ref, o_ref, *, d_qk, heads_per_step, kv_per_step):
    qts = [q_ref[:, j * d_qk:(j + 1) * d_qk].T for j in range(heads_per_step)]
    kv_of = lambda j: j if kv_per_step > 1 else 0

    def score(c, j):
        jk = kv_of(j)
        return _dot(k_ref[_chunk_rows(c, TK), jk * d_qk:(jk + 1) * d_qk], qts[j])

    def values(c, j):
        return vt_ref[c, kv_of(j) * V_ROWS:(kv_of(j) + 1) * V_ROWS, :]

    def emit(accs):
        for j in range(heads_per_step):
            o = _normalised(accs[j]) * g_ref[:, j * HEAD_W:(j + 1) * HEAD_W]
            o_ref[:, j * HEAD_W:(j + 1) * HEAD_W] = o.astype(BF16)

    accs, overflowed = _stream_softmax(heads_per_step, score, values, N_CHUNKS, TQ_PLAIN)
    emit(accs)

    @pl.when(overflowed)
    def _():
        emit(_exact_softmax(heads_per_step, score, values, 0, N_CHUNKS, TQ_PLAIN))


def _gated_attn(q, k, vt, gate, *, d_qk, q_heads_per_kv, heads_per_step, gate_block0, name):
    b, s, _ = q.shape
    hs = heads_per_step
    kv_per_step = max(1, hs // q_heads_per_kv)
    steps_per_kv = max(1, q_heads_per_kv // hs)
    gate0 = gate_block0 // hs
    return pl.pallas_call(
        functools.partial(_gated_attn_kernel, d_qk=d_qk, heads_per_step=hs, kv_per_step=kv_per_step),
        grid=(b, HEADS // hs, s // TQ_PLAIN),
        in_specs=[
            pl.BlockSpec((None, TQ_PLAIN, hs * d_qk), lambda bb, h, qi: (bb, qi, h)),
            pl.BlockSpec((None, s, kv_per_step * d_qk), lambda bb, h, qi: (bb, 0, h // steps_per_kv)),
            pl.BlockSpec((N_CHUNKS, kv_per_step * V_ROWS, TK), lambda bb, h, qi: (bb, h // steps_per_kv, 0)),
            pl.BlockSpec((None, TQ_PLAIN, hs * HEAD_W), lambda bb, h, qi: (bb, qi, gate0 + h)),
        ],
        out_specs=pl.BlockSpec((None, TQ_PLAIN, hs * HEAD_W), lambda bb, h, qi: (bb, qi, h)),
        out_shape=jax.ShapeDtypeStruct((b, s, HEADS * HEAD_W), BF16),
        compiler_params=pltpu.CompilerParams(
            dimension_semantics=("arbitrary",) * 3, vmem_limit_bytes=VMEM_LIMIT),
        name=name,
    )(q, k, vt, gate)


def _outproj_kernel(oa_ref, ob_ref, w_ref, x_ref, pg_ref, y_ref):
    half = oa_ref.shape[-1]
    m = _dot(oa_ref[...], w_ref[:half, :]) + _dot(ob_ref[...], w_ref[half:, :])
    y_ref[...] = x_ref[...] + _rms(m, pg_ref[...])


def _outproj(oa, ob, w, x2, post_gain, name):
    m = x2.shape[0]
    tm = TM_OUT
    row = lambda i: (i, 0)
    fixed = lambda i: (0, 0)
    return pl.pallas_call(
        _outproj_kernel,
        grid=(m // tm,),
        in_specs=[
            pl.BlockSpec((tm, oa.shape[-1]), row),
            pl.BlockSpec((tm, ob.shape[-1]), row),
            pl.BlockSpec(w.shape, fixed, pipeline_mode=pl.Buffered(1)),
            pl.BlockSpec((tm, D_MODEL), row),
            pl.BlockSpec((1, D_MODEL), fixed),
        ],
        out_specs=pl.BlockSpec((tm, D_MODEL), row),
        out_shape=jax.ShapeDtypeStruct((m, D_MODEL), F32),
        compiler_params=pltpu.CompilerParams(
            dimension_semantics=("arbitrary",), vmem_limit_bytes=VMEM_LIMIT),
        name=name,
    )(oa, ob, w, x2, post_gain)


_O_QKV, _O_GC, _O_QD, _O_KD, _O_GD, _O_COLS = 0, 3072, 4096, 5120, 5632, 6656


def _odd_inproj_kernel(x_ref, pg_ref, w_ref, wvt_ref, qn_ref, kn_ref, cos_ref, sin_ref,
                       qkv_ref, vt_ref, gate_ref, qd_ref, kd_ref):
    h = _rms(x_ref[...], pg_ref[...]).astype(BF16)
    cos = cos_ref[...]
    sin = sin_ref[...]
    scale = HEAD_W ** -0.5 * LOG2E

    def proj(c0, n):
        return _dot(h, w_ref[:, c0:c0 + n])

    for c in range(0, 1024, _N_CHUNK):
        qkv_ref[:, c:c + _N_CHUNK] = (proj(_O_QKV + c, _N_CHUNK) * scale).astype(BF16)
    for c in range(1024, 3072, _N_CHUNK):
        qkv_ref[:, c:c + _N_CHUNK] = proj(_O_QKV + c, _N_CHUNK).astype(BF16)
    _store_vt(vt_ref, _dot_nt(wvt_ref[...], h), D_KV_HEADS)
    for c in range(0, 1024, _N_CHUNK):
        gate_ref[:, c:c + _N_CHUNK] = _silu(proj(_O_GC + c, _N_CHUNK))
        gate_ref[:, 1024 + c:1024 + c + _N_CHUNK] = _silu(proj(_O_GD + c, _N_CHUNK))
    for c in range(0, 1024, _N_CHUNK):
        qd = proj(_O_QD + c, _N_CHUNK)
        for j in range(0, _N_CHUNK, HEAD_W):
            qh = _rope(_rms(qd[:, j:j + HEAD_W], qn_ref[...]), cos, sin) * scale
            qd_ref[:, c + j:c + j + HEAD_W] = qh.astype(BF16)
    kd = proj(_O_KD, 256)
    for j in range(0, 256, HEAD_W):
        kh = _rope(_rms(kd[:, j:j + HEAD_W], kn_ref[...]), cos, sin)
        kd_ref[:, j:j + HEAD_W] = kh.astype(BF16)


def _odd_inproj(x2, pre_gain, w_perm, wvt, q_norm, k_norm, cos_ax, sin_ax):
    m = x2.shape[0]
    tm = TM_PROJ
    pos_blocks = SEQ // tm
    row = lambda i: (i, 0)
    fixed = lambda i: (0, 0)
    pos = lambda i: (i % pos_blocks, 0)
    return pl.pallas_call(
        _odd_inproj_kernel,
        grid=(m // tm,),
        in_specs=[
            pl.BlockSpec((tm, D_MODEL), row),
            pl.BlockSpec((1, D_MODEL), fixed),
            pl.BlockSpec((D_MODEL, _O_COLS), fixed, pipeline_mode=pl.Buffered(1)),
            pl.BlockSpec((256, D_MODEL), fixed, pipeline_mode=pl.Buffered(1)),
            pl.BlockSpec((1, HEAD_W), fixed),
            pl.BlockSpec((1, HEAD_W), fixed),
            pl.BlockSpec((tm, HEAD_W), pos),
            pl.BlockSpec((tm, HEAD_W), pos),
        ],
        out_specs=[
            pl.BlockSpec((tm, 3072), row),
            _vt_out_spec(D_KV_HEADS, tm),
            pl.BlockSpec((tm, 2048), row),
            pl.BlockSpec((tm, 1024), row),
            pl.BlockSpec((tm, 256), row),
        ],
        out_shape=[
            jax.ShapeDtypeStruct((m, 3072), BF16),
            _vt_shape(m, D_KV_HEADS),
            jax.ShapeDtypeStruct((m, 2048), F32),
            jax.ShapeDtypeStruct((m, 1024), BF16),
            jax.ShapeDtypeStruct((m, 256), BF16),
        ],
        compiler_params=pltpu.CompilerParams(
            dimension_semantics=("arbitrary",), vmem_limit_bytes=VMEM_LIMIT),
        name="odd_inproj",
    )(x2, pre_gain, w_perm, wvt, q_norm, k_norm, cos_ax, sin_ax)


_NA_Q = NA_GROUP_ROWS * GRID_W
_NA_K = NA_BAND_ROWS * GRID_W
_NA_GROUPS = SEQ // _NA_Q
_NA_ROWS = SEQ // GRID_W
_NA_LAST_START = _NA_ROWS - NA_BAND_ROWS
_NA_CLASS_GROUPS = (0, 1, _NA_GROUPS - 1)


def _na_band_start(g):
    lo = g * NA_GROUP_ROWS - NA_KH // 2
    if isinstance(g, int):
        return min(max(lo, 0), _NA_LAST_START)
    return jnp.clip(lo, 0, _NA_LAST_START)


def _na_build_bias(rpb_ref, bias_ref):
    vec = rpb_ref[...] * LOG2E
    even_src = pltpu.roll(vec, 128 - (NA_KW - 1), 1)
    odd_src = pltpu.roll(vec, GRID_W - (NA_KW - 1), 1)
    q_idx = lax.broadcasted_iota(jnp.int32, (GRID_W, 128), 0)
    lane = lax.broadcasted_iota(jnp.int32, (GRID_W, 128), 1)
    w_idx = lane & (GRID_W - 1)
    c0 = jnp.clip(q_idx - NA_KW // 2, 0, GRID_W - NA_KW)
    col_in = (w_idx >= c0) & (w_idx < c0 + NA_KW)
    neg = jnp.full((GRID_W, 128), NEG_INF, F32)
    cache = {}
    for cls, g in enumerate(_NA_CLASS_GROUPS):
        start = _na_band_start(g)
        for qr in range(NA_GROUP_ROWS):
            rq = g * NA_GROUP_ROWS + qr
            r0 = min(max(rq - NA_KH // 2, 0), _NA_ROWS - NA_KH)
            for kp in range(NA_BAND_ROWS // 2):
                halves = []
                for rk in (start + 2 * kp, start + 2 * kp + 1):
                    halves.append(rk - rq + NA_KH - 1 if r0 <= rk < r0 + NA_KH else None)
                key = tuple(halves)
                if key not in cache:
                    if halves[0] is None and halves[1] is None:
                        cache[key] = neg
                    else:
                        src = jnp.zeros((1, 128), F32)
                        mask = None
                        if halves[0] is not None:
                            src = src + even_src[halves[0]:halves[0] + 1, :]
                            mask = lane < GRID_W
                        if halves[1] is not None:
                            src = src + odd_src[halves[1]:halves[1] + 1, :]
                            mask = (lane >= GRID_W) if mask is None else None
                        tile = pltpu.roll(jnp.broadcast_to(src, (GRID_W, 128)), 0, 1, stride=1, stride_axis=0)
                        keep = col_in if mask is None else (col_in & mask)
                        cache[key] = jnp.where(keep, tile, neg)
                bias_ref[cls, qr * GRID_W:(qr + 1) * GRID_W, kp * 128:(kp + 1) * 128] = cache[key]


def _na_kernel(q_ref, k_ref, v_ref, rpb_ref, g_ref, o_ref, bias_ref):
    g = pl.program_id(2)

    @pl.when(g == 0)
    def _():
        for j in range(NA_HEADS_PER_STEP):
            _na_build_bias(rpb_ref.at[j], bias_ref.at[j])

    cls = jnp.where(g == 0, 0, jnp.where(g == _NA_GROUPS - 1, 2, 1))
    start = pl.multiple_of(_na_band_start(g) * GRID_W, GRID_W)
    heads = [slice(j * HEAD_W, (j + 1) * HEAD_W) for j in range(NA_HEADS_PER_STEP)]
    scores = [_dot_nt(q_ref[:, hd], k_ref[pl.ds(start, _NA_K), hd]) + bias_ref[j, cls]
              for j, hd in enumerate(heads)]
    for hd, s in zip(heads, scores):
        m = jnp.max(s, axis=1, keepdims=True)
        p = jnp.exp2(s - m)
        l = jnp.sum(p, axis=1, keepdims=True)
        o = _dot(p.astype(BF16), v_ref[pl.ds(start, _NA_K), hd]) * (1.0 / l)
        o_ref[:, hd] = (o * g_ref[:, hd]).astype(BF16)


def _na_attn(qkv, gate, rpb_pad):
    b, s, _ = qkv.shape
    hs = NA_HEADS_PER_STEP
    steps = HEADS // hs
    return pl.pallas_call(
        _na_kernel,
        grid=(b, steps, _NA_GROUPS),
        in_specs=[
            pl.BlockSpec((None, _NA_Q, hs * HEAD_W), lambda bb, h, g: (bb, g, h)),
            pl.BlockSpec((None, s, hs * HEAD_W), lambda bb, h, g: (bb, 0, steps + h)),
            pl.BlockSpec((None, s, hs * HEAD_W), lambda bb, h, g: (bb, 0, 2 * steps + h)),
            pl.BlockSpec((hs, 16, 128), lambda bb, h, g: (h, 0, 0)),
            pl.BlockSpec((None, _NA_Q, hs * HEAD_W), lambda bb, h, g: (bb, g, h)),
        ],
        out_specs=pl.BlockSpec((None, _NA_Q, hs * HEAD_W), lambda bb, h, g: (bb, g, h)),
        out_shape=jax.ShapeDtypeStruct((b, s, HEADS * HEAD_W), BF16),
        scratch_shapes=[pltpu.VMEM((hs, 3, _NA_Q, _NA_K), F32)],
        compiler_params=pltpu.CompilerParams(
            dimension_semantics=("arbitrary",) * 3, vmem_limit_bytes=VMEM_LIMIT),
        name="na_attn",
    )(qkv, qkv, qkv, rpb_pad, gate)


def _rope_seg_tables(pos):
    inv = ROPE_THETA ** (-np.arange(ROPE_HALF, dtype=np.float64) / ROPE_HALF)
    ang = pos.astype(np.float64)[:, None] * inv[None, :]
    cos, sin = np.cos(ang), np.sin(ang)
    return np.concatenate([cos, cos], axis=1), np.concatenate([-sin, sin], axis=1)


def _rope_tables():
    t = np.arange(SEQ)
    cos_t, sin_t = _rope_seg_tables(t)
    ones, zeros = np.ones((SEQ, ROPE_SEG)), np.zeros((SEQ, ROPE_SEG))
    cos_k = np.concatenate([cos_t, ones], axis=1)
    sin_k = np.concatenate([sin_t, zeros], axis=1)
    cos_q = np.concatenate([ones, ones, cos_t, ones], axis=1)
    sin_q = np.concatenate([zeros, zeros, sin_t, zeros], axis=1)
    cos_r, sin_r = _rope_seg_tables(t // GRID_W)
    cos_c, sin_c = _rope_seg_tables(t % GRID_W)
    cos_ax = np.concatenate([cos_r, cos_c], axis=1)
    sin_ax = np.concatenate([sin_r, sin_c], axis=1)
    as_f32 = lambda a: jnp.asarray(a, F32)
    return tuple(map(as_f32, (cos_k, sin_k, cos_q, sin_q, cos_ax, sin_ax)))


def _transpose_cols_kernel(w_ref, o_ref):
    o_ref[...] = w_ref[...].T.astype(BF16)


def _transpose_cols(w, col0, width):
    k = w.shape[0]
    rows = 256
    assert col0 % width == 0 and k % rows == 0
    return pl.pallas_call(
        _transpose_cols_kernel,
        grid=(k // rows,),
        in_specs=[pl.BlockSpec((rows, width), lambda i: (i, col0 // width))],
        out_specs=pl.BlockSpec((width, rows), lambda i: (0, i)),
        out_shape=jax.ShapeDtypeStruct((width, k), BF16),
        name="transpose_cols",
    )(w)


def _even_weights(w_in, w_uq, w_ukv):
    w_bf = w_in.astype(BF16)
    va_t = _transpose_cols(w_in, 2048, 1024)
    wq = w_uq.reshape(B_Q_RANK, HEADS, B_NOPE + B_ROPE)
    wq = jnp.pad(wq, ((0, 0), (0, 0), (0, B_QK_PAD - B_NOPE - B_ROPE))).reshape(B_Q_RANK, HEADS * B_QK_PAD)
    wkv = w_ukv.reshape(B_KV_RANK, HEADS, B_NOPE + HEAD_W)
    wk = wkv[:, :, :B_NOPE].reshape(B_KV_RANK, HEADS * B_NOPE)
    wv = wkv[:, :, B_NOPE:].reshape(B_KV_RANK, HEADS * HEAD_W)
    return (w_bf, va_t, wq.astype(BF16), wk.astype(BF16), wv.T.astype(BF16))


def _odd_weights(w_in):
    return w_in.astype(BF16), _transpose_cols(w_in, 5376, 256)


def _even_layer(x2, batch, layer, pre_gain, post_gain, w_in, w_out, lq1, lk1, lq2, lk2, subln,
                q_norm, w_uq, kv_norm, w_ukv, tables):
    cos_k, sin_k, cos_q, sin_q, _, _ = tables
    lambda_init = 0.8 - 0.6 * math.exp(-0.3 * layer)
    w_perm, wvat, wq, wk, wvbt = _even_weights(w_in, w_uq, w_ukv)
    qk, vat, gate, cq, ckv, kpe = _even_inproj(
        x2, pre_gain[None], w_perm, wvat, q_norm[None], kv_norm[None], cos_k, sin_k)
    as3 = lambda a: a.reshape(batch, SEQ, a.shape[-1])
    slopes = jnp.asarray(2.0 ** (-8.0 * np.arange(1, HEADS + 1) / HEADS), F32)
    qaug, kaug = _alibi_aug_tables()
    oa = _diff_attn(slopes, as3(qk), vat, as3(gate), qaug, kaug, lq1[None], lk1[None], lq2[None],
                    lk2[None], subln[None], lambda_init)
    q_b, k_b, vbt = _mla_up(cq, ckv, kpe, wq, wk, wvbt, cos_q, sin_q)
    ob = _gated_attn(as3(q_b), as3(k_b), vbt, as3(gate), d_qk=B_QK_PAD, q_heads_per_kv=1,
                     heads_per_step=2, gate_block0=HEADS, name="latent_attn")
    m = x2.shape[0]
    return _outproj(oa.reshape(m, -1), ob.reshape(m, -1), w_out.astype(BF16), x2, post_gain[None],
                    "even_outproj")


def _odd_layer(x2, batch, pre_gain, post_gain, w_in, w_out, rpb, q_norm, k_norm, tables):
    cos_ax, sin_ax = tables[4], tables[5]
    w_perm, wvdt = _odd_weights(w_in)
    qkv, vdt, gate, qd, kd = _odd_inproj(
        x2, pre_gain[None], w_perm, wvdt, q_norm[None], k_norm[None], cos_ax, sin_ax)
    as3 = lambda a: a.reshape(batch, SEQ, a.shape[-1])
    rpb_pad = jnp.pad(rpb, ((0, 0), (0, 16 - rpb.shape[1]), (0, 128 - rpb.shape[2])))
    oc = _na_attn(as3(qkv), as3(gate), rpb_pad)
    od = _gated_attn(as3(qd), as3(kd), vdt, as3(gate), d_qk=HEAD_W,
                     q_heads_per_kv=HEADS // D_KV_HEADS, heads_per_step=2, gate_block0=HEADS,
                     name="gqa_attn")
    m = x2.shape[0]
    return _outproj(oc.reshape(m, -1), od.reshape(m, -1), w_out.astype(BF16), x2, post_gain[None],
                    "odd_outproj")


def kernel(x, pre_norm, post_norm, even_w_in, even_w_out, diff_lambda_q1, diff_lambda_k1,
           diff_lambda_q2, diff_lambda_k2, diff_subln, mla_q_norm, mla_w_uq, mla_kv_norm, mla_w_ukv,
           odd_w_in, odd_w_out, na_rpb, gqa_q_norm, gqa_k_norm):
    batch, seq, d_model = x.shape
    assert (seq, d_model) == (SEQ, D_MODEL)
    depth = pre_norm.shape[0]
    tables = _rope_tables()
    x2 = x.reshape(batch * seq, d_model)
    for layer in range(depth):
        i = layer // 2
        if layer % 2 == 0:
            x2 = _even_layer(x2, batch, layer, pre_norm[layer], post_norm[layer], even_w_in[i],
                             even_w_out[i], diff_lambda_q1[i], diff_lambda_k1[i], diff_lambda_q2[i],
                             diff_lambda_k2[i], diff_subln[i], mla_q_norm[i], mla_w_uq[i],
                             mla_kv_norm[i], mla_w_ukv[i], tables)
        else:
            x2 = _odd_layer(x2, batch, pre_norm[layer], post_norm[layer], odd_w_in[i], odd_w_out[i],
                            na_rpb[i], gqa_q_norm[i], gqa_k_norm[i], tables)
    return x2.reshape(batch, seq, d_model)
```

```python
import functools
import math

import numpy as np
import jax
import jax.numpy as jnp
from jax import lax
from jax.experimental import pallas as pl
from jax.experimental.pallas import tpu as pltpu

F32 = jnp.float32
BF16 = jnp.bfloat16

D_MODEL = 2048
SEQ = 4096
GRID_W = 64
NORM_EPS = 1e-6
ROPE_THETA = 10000.0
NEG_INF = -1e30
LOG2E = math.log2(math.e)

HEADS = 8
HEAD_W = 128
V_ROWS = HEAD_W + 16
A_DQK = 64
B_NOPE = 128
B_ROPE = 64
B_Q_RANK = 768
B_KV_RANK = 512
B_QK_PAD = 256
D_KV_HEADS = 2
NA_KH = 8
NA_KW = 16
NA_GROUP_ROWS = 4
NA_BAND_ROWS = 12
NA_HEADS_PER_STEP = 4

ROPE_SEG = 64
ROPE_HALF = 32

VMEM_LIMIT = 56 * 1024 * 1024

TM_PROJ = 256
TM_OUT = 512
TK = 512
N_CHUNKS = SEQ // TK
TQ_DIFF = 1024
_DIAG_CHUNKS = TQ_DIFF // TK
TQ_PLAIN = 2048


def _rms(xf, gain):
    ms = jnp.mean(xf * xf, axis=-1, keepdims=True)
    return xf * lax.rsqrt(ms + NORM_EPS) * gain


def _silu(g):
    return g * (1.0 / (1.0 + jnp.exp(-g)))


def _rope(x, cos, sin_signed):
    w = x.shape[-1]
    lane = lax.broadcasted_iota(jnp.int32, x.shape, 1)
    first_half = (lane & (ROPE_SEG - 1)) < ROPE_HALF
    partner = jnp.where(first_half, pltpu.roll(x, w - ROPE_HALF, 1), pltpu.roll(x, ROPE_HALF, 1))
    return x * cos + partner * sin_signed


def _dot(a, b):
    return jnp.dot(a, b, preferred_element_type=F32)


def _dot_nt(a, b):
    return lax.dot_general(a, b, (((1,), (1,)), ((), ())), preferred_element_type=F32)


def _sampled_max(st):
    tiles = [st[i:i + 8, :] for i in range(0, st.shape[0], 32)]
    return jnp.max(functools.reduce(jnp.maximum, tiles), axis=0, keepdims=True)


def _stream_step(carry, st, vt):
    r, acc = carry
    return r, acc + _dot(vt, jnp.exp2(st - r).astype(BF16))


def _stream_softmax(n_streams, score_fn, vt_fn, n_steps, tq, n_init=1):
    ahead = [[score_fn(s, j) for j in range(n_streams)] for s in range(n_init)]
    carries = []
    for j in range(n_streams):
        r = functools.reduce(jnp.maximum, [_sampled_max(ahead[s][j]) for s in range(n_init)])
        carries.append((r, jnp.zeros((V_ROWS, tq), F32)))
    for step in range(n_steps):
        if step + n_init < n_steps:
            ahead.append([score_fn(step + n_init, j) for j in range(n_streams)])
        cur = ahead.pop(0)
        for j in range(n_streams):
            carries[j] = _stream_step(carries[j], cur[j], vt_fn(step, j))
    accs = [c[1] for c in carries]
    finite = [jnp.min(jnp.where(jnp.isfinite(a), 1.0, 0.0)) for a in accs]
    return accs, functools.reduce(jnp.minimum, finite) < 0.5


def _exact_step(carry, st, vt):
    m, acc = carry
    m_new = jnp.maximum(m, jnp.max(st, axis=0, keepdims=True))
    p = jnp.exp2(st - m_new).astype(BF16)
    return m_new, jnp.exp2(m - m_new) * acc + _dot(vt, p)


def _exact_softmax(n_streams, score_fn, vt_fn, n_static, n_steps, tq):
    carries = [(jnp.full((1, tq), NEG_INF, F32), jnp.zeros((V_ROWS, tq), F32)) for _ in range(n_streams)]
    for step in range(n_static):
        carries = [_exact_step(carries[j], score_fn(step, j), vt_fn(step, j)) for j in range(n_streams)]

    def body(step, carries):
        return tuple(_exact_step(carries[j], score_fn(step, j), vt_fn(step, j)) for j in range(n_streams))

    carries = lax.fori_loop(n_static, n_steps, body, tuple(carries))
    return [c[1] for c in carries]


def _normalised(acc):
    l = acc[HEAD_W:HEAD_W + 1, :]
    return (acc[:HEAD_W, :] * (1.0 / l)).T


def _chunk_rows(c, n):
    if isinstance(c, int):
        return slice(c * n, (c + 1) * n)
    return pl.ds(pl.multiple_of(c * n, n), n)


def _store_vt(vt_ref, vt, n_heads):
    ones = jnp.ones((V_ROWS - HEAD_W, vt.shape[1]), BF16)
    for h in range(n_heads):
        vt_ref[h * V_ROWS:h * V_ROWS + HEAD_W, :] = vt[h * HEAD_W:(h + 1) * HEAD_W, :].astype(BF16)
        vt_ref[h * V_ROWS + HEAD_W:(h + 1) * V_ROWS, :] = ones


_E_QK, _E_GA, _E_CQ, _E_CKV, _E_KR, _E_COLS = 0, 3072, 4096, 4864, 5376, 6464
_N_CHUNK = 512


def _even_inproj_kernel(x_ref, pg_ref, w_ref, wvt_ref, qn_ref, kvn_ref, cos_ref, sin_ref,
                        qk_ref, vt_ref, gate_ref, cq_ref, ckv_ref, kpe_ref):
    h = _rms(x_ref[...], pg_ref[...]).astype(BF16)

    def proj(c0, n):
        return _dot(h, w_ref[:, c0:c0 + n])

    q_scale = (A_DQK ** -0.5) * LOG2E
    for c in range(0, 1024, _N_CHUNK):
        qk_ref[:, c:c + _N_CHUNK] = (proj(_E_QK + c, _N_CHUNK) * q_scale).astype(BF16)
    for c in range(1024, 2048, _N_CHUNK):
        qk_ref[:, c:c + _N_CHUNK] = proj(_E_QK + c, _N_CHUNK).astype(BF16)
    _store_vt(vt_ref, _dot_nt(wvt_ref[...], h), HEADS)
    for c in range(0, 1024, _N_CHUNK):
        gate_ref[:, c:c + _N_CHUNK] = _silu(proj(_E_GA + c, _N_CHUNK))
    cq_ref[...] = _rms(proj(_E_CQ, B_Q_RANK), qn_ref[...]).astype(BF16)
    ckv_ref[...] = _rms(proj(_E_CKV, B_KV_RANK), kvn_ref[...]).astype(BF16)
    tail = proj(_E_KR, _E_COLS - _E_KR)
    gate_ref[:, 1024:2048] = _silu(tail[:, B_ROPE:])
    kr = tail[:, :128]
    lane = lax.broadcasted_iota(jnp.int32, kr.shape, 1)
    kpe = jnp.where(lane < B_ROPE, _rope(kr, cos_ref[...], sin_ref[...]), 0.0)
    kpe_ref[...] = kpe.astype(BF16)


def _vt_out_spec(n_heads, tm):
    per_chunk = TK // tm
    return pl.BlockSpec((None, n_heads * V_ROWS, tm), lambda i: (i // per_chunk, 0, i % per_chunk))


def _vt_shape(m, n_heads):
    return jax.ShapeDtypeStruct((m // TK, n_heads * V_ROWS, TK), BF16)


def _even_inproj(x2, pre_gain, w_perm, wvt, q_norm, kv_norm, cos_k, sin_k):
    m = x2.shape[0]
    tm = TM_PROJ
    pos_blocks = SEQ // tm
    row = lambda i: (i, 0)
    fixed = lambda i: (0, 0)
    pos = lambda i: (i % pos_blocks, 0)
    return pl.pallas_call(
        _even_inproj_kernel,
        grid=(m // tm,),
        in_specs=[
            pl.BlockSpec((tm, D_MODEL), row),
            pl.BlockSpec((1, D_MODEL), fixed),
            pl.BlockSpec((D_MODEL, _E_COLS), fixed, pipeline_mode=pl.Buffered(1)),
            pl.BlockSpec((1024, D_MODEL), fixed, pipeline_mode=pl.Buffered(1)),
            pl.BlockSpec((1, B_Q_RANK), fixed),
            pl.BlockSpec((1, B_KV_RANK), fixed),
            pl.BlockSpec((tm, 128), pos),
            pl.BlockSpec((tm, 128), pos),
        ],
        out_specs=[
            pl.BlockSpec((tm, 2048), row),
            _vt_out_spec(HEADS, tm),
            pl.BlockSpec((tm, 2048), row),
            pl.BlockSpec((tm, B_Q_RANK), row),
            pl.BlockSpec((tm, B_KV_RANK), row),
            pl.BlockSpec((tm, 128), row),
        ],
        out_shape=[
            jax.ShapeDtypeStruct((m, 2048), BF16),
            _vt_shape(m, HEADS),
            jax.ShapeDtypeStruct((m, 2048), F32),
            jax.ShapeDtypeStruct((m, B_Q_RANK), BF16),
            jax.ShapeDtypeStruct((m, B_KV_RANK), BF16),
            jax.ShapeDtypeStruct((m, 128), BF16),
        ],
        compiler_params=pltpu.CompilerParams(
            dimension_semantics=("arbitrary",), vmem_limit_bytes=VMEM_LIMIT),
        name="even_inproj",
    )(x2, pre_gain, w_perm, wvt, q_norm, kv_norm, cos_k, sin_k)


def _log2e_pieces():
    pieces, rest = [], np.float64(LOG2E)
    for _ in range(3):
        p = np.float64(np.asarray(rest).astype(BF16))
        pieces.append(p)
        rest = rest - p
    return pieces


def _alibi_aug_tables():
    pos = np.arange(SEQ)
    hi, lo = (pos // 64) * 64.0, (pos % 64) * 1.0
    c = _log2e_pieces()
    qaug = np.zeros((SEQ, HEAD_W))
    kaug = np.zeros((SEQ, HEAD_W))
    for p in range(3):
        qaug[:, p], kaug[:, p] = c[p], hi
        qaug[:, 3 + p], kaug[:, 3 + p] = c[p], lo
        qaug[:, 6 + p], kaug[:, 6 + p] = -hi, c[p]
        qaug[:, 9 + p], kaug[:, 9 + p] = -lo, c[p]
    return jnp.asarray(qaug, F32), jnp.asarray(kaug, BF16)


def _diff_attn_kernel(slopes_ref, q_ref, k_ref, vt_ref, g_ref, qaug_ref, kaug_ref,
                      lq1_ref, lk1_ref, lq2_ref, lk2_ref, subln_ref, o_ref, kcat_ref, dbias_ref,
                      *, lambda_init):
    head = pl.program_id(1)
    qi = pl.program_id(2)
    slope = slopes_ref[head]

    @pl.when(qi == 0)
    def _():
        k = k_ref[...]
        kaug = kaug_ref[...]
        kcat_ref[0, :, :HEAD_W] = k
        kcat_ref[0, :, HEAD_W:] = kaug
        kcat_ref[1, :, :HEAD_W] = k
        kcat_ref[1, :, HEAD_W:] = -kaug
        key = lax.broadcasted_iota(jnp.int32, (TK, TQ_DIFF), 0)
        qry = lax.broadcasted_iota(jnp.int32, (TK, TQ_DIFF), 1)
        for r in range(_DIAG_CHUNKS):
            dbias_ref[r] = jnp.abs(qry - key - r * TK).astype(F32) * (-slope * LOG2E)

    q = q_ref[...]
    lane = lax.broadcasted_iota(jnp.int32, q.shape, 1)
    zero = jnp.zeros_like(q)
    q_maps = (jnp.where(lane < A_DQK, q, zero), jnp.where(lane >= A_DQK, q, zero))
    qaug = (qaug_ref[...] * slope).astype(BF16)
    q_diag = [jnp.concatenate([qm, zero], axis=1) for qm in q_maps]
    q_off = [jnp.concatenate([qm, qaug], axis=1) for qm in q_maps]

    def chunk_of(r):
        first = qi * _DIAG_CHUNKS
        wrapped = (first + r >= N_CHUNKS).astype(jnp.int32)
        return first + r - wrapped * N_CHUNKS, wrapped

    def score(r, mi):
        c, wrapped = chunk_of(r)
        kc = kcat_ref[1 - wrapped, pl.ds(pl.multiple_of(c * TK, TK), TK), :]
        if isinstance(r, int) and r < _DIAG_CHUNKS:
            return _dot_nt(kc, q_diag[mi]) + dbias_ref[r]
        return _dot_nt(kc, q_off[mi])

    def values(r, mi):
        return vt_ref[chunk_of(r)[0]]

    lam = (jnp.exp(jnp.sum(lq1_ref[...] * lk1_ref[...], axis=-1, keepdims=True))
           - jnp.exp(jnp.sum(lq2_ref[...] * lk2_ref[...], axis=-1, keepdims=True))
           + lambda_init)

    def emit(accs):
        o = _normalised(accs[0]) - lam * _normalised(accs[1])
        o = _rms(o, subln_ref[...]) * (1.0 - lambda_init)
        o_ref[...] = (o * g_ref[...]).astype(BF16)

    accs, overflowed = _stream_softmax(2, score, values, N_CHUNKS, TQ_DIFF, n_init=_DIAG_CHUNKS)
    emit(accs)

    @pl.when(overflowed)
    def _():
        emit(_exact_softmax(2, score, values, _DIAG_CHUNKS, N_CHUNKS, TQ_DIFF))


def _diff_attn(slopes, qk, vt, gate, qaug, kaug, lq1, lk1, lq2, lk2, subln, lambda_init):
    b, s, _ = qk.shape
    vec = lambda bb, h, qi: (0, 0)
    return pl.pallas_call(
        functools.partial(_diff_attn_kernel, lambda_init=lambda_init),
        grid=(b, HEADS, s // TQ_DIFF),
        in_specs=[
            pl.BlockSpec(memory_space=pltpu.SMEM),
            pl.BlockSpec((None, TQ_DIFF, HEAD_W), lambda bb, h, qi: (bb, qi, h)),
            pl.BlockSpec((None, s, HEAD_W), lambda bb, h, qi: (bb, 0, HEADS + h)),
            pl.BlockSpec((N_CHUNKS, V_ROWS, TK), lambda bb, h, qi: (bb, h, 0)),
            pl.BlockSpec((None, TQ_DIFF, HEAD_W), lambda bb, h, qi: (bb, qi, h)),
            pl.BlockSpec((TQ_DIFF, HEAD_W), lambda bb, h, qi: (qi, 0)),
            pl.BlockSpec((s, HEAD_W), vec),
            pl.BlockSpec((1, A_DQK), vec),
            pl.BlockSpec((1, A_DQK), vec),
            pl.BlockSpec((1, A_DQK), vec),
            pl.BlockSpec((1, A_DQK), vec),
            pl.BlockSpec((1, HEAD_W), vec),
        ],
        out_specs=pl.BlockSpec((None, TQ_DIFF, HEAD_W), lambda bb, h, qi: (bb, qi, h)),
        out_shape=jax.ShapeDtypeStruct((b, s, HEADS * HEAD_W), BF16),
        scratch_shapes=[pltpu.VMEM((2, s, 2 * HEAD_W), BF16),
                        pltpu.VMEM((_DIAG_CHUNKS, TK, TQ_DIFF), F32)],
        compiler_params=pltpu.CompilerParams(
            dimension_semantics=("arbitrary",) * 3, vmem_limit_bytes=VMEM_LIMIT),
        name="diff_attn",
    )(slopes, qk, qk, vt, gate, qaug, kaug, lq1, lk1, lq2, lk2, subln)


def _mla_up_kernel(cq_ref, ckv_ref, kpe_ref, wq_ref, wk_ref, wvt_ref, cos_ref, sin_ref,
                   q_ref, k_ref, vt_ref):
    cq = cq_ref[...]
    ckv = ckv_ref[...]
    kpe = kpe_ref[...]
    cos = cos_ref[...]
    sin = sin_ref[...]
    scale = (B_NOPE + B_ROPE) ** -0.5 * LOG2E
    for h in range(HEADS):
        qh = _dot(cq, wq_ref[:, h * B_QK_PAD:(h + 1) * B_QK_PAD])
        q_ref[:, h * B_QK_PAD:(h + 1) * B_QK_PAD] = (_rope(qh, cos, sin) * scale).astype(BF16)
        k_ref[:, h * B_QK_PAD + B_NOPE:(h + 1) * B_QK_PAD] = kpe
    for h in range(0, HEADS, 2):
        kk = _dot(ckv, wk_ref[:, h * B_NOPE:(h + 2) * B_NOPE]).astype(BF16)
        k_ref[:, h * B_QK_PAD:h * B_QK_PAD + B_NOPE] = kk[:, :B_NOPE]
        k_ref[:, (h + 1) * B_QK_PAD:(h + 1) * B_QK_PAD + B_NOPE] = kk[:, B_NOPE:]
    _store_vt(vt_ref, _dot_nt(wvt_ref[...], ckv), HEADS)


def _mla_up(cq, ckv, kpe, wq, wk, wvt, cos_q, sin_q):
    m = cq.shape[0]
    tm = TK
    pos_blocks = SEQ // tm
    row = lambda i: (i, 0)
    fixed = lambda i: (0, 0)
    pos = lambda i: (i % pos_blocks, 0)
    return pl.pallas_call(
        _mla_up_kernel,
        grid=(m // tm,),
        in_specs=[
            pl.BlockSpec((tm, B_Q_RANK), row),
            pl.BlockSpec((tm, B_KV_RANK), row),
            pl.BlockSpec((tm, 128), row),
            pl.BlockSpec(wq.shape, fixed),
            pl.BlockSpec(wk.shape, fixed),
            pl.BlockSpec(wvt.shape, fixed),
            pl.BlockSpec((tm, B_QK_PAD), pos),
            pl.BlockSpec((tm, B_QK_PAD), pos),
        ],
        out_specs=[
            pl.BlockSpec((tm, HEADS * B_QK_PAD), row),
            pl.BlockSpec((tm, HEADS * B_QK_PAD), row),
            _vt_out_spec(HEADS, tm),
        ],
        out_shape=[
            jax.ShapeDtypeStruct((m, HEADS * B_QK_PAD), BF16),
            jax.ShapeDtypeStruct((m, HEADS * B_QK_PAD), BF16),
            _vt_shape(m, HEADS),
        ],
        compiler_params=pltpu.CompilerParams(
            dimension_semantics=("arbitrary",), vmem_limit_bytes=VMEM_LIMIT),
        name="mla_up",
    )(cq, ckv, kpe, wq, wk, wvt, cos_q, sin_q)


def _gated_attn_kernel(q_ref, k_ref, vt_ref, g_ref, o_ref, *, d_qk, heads_per_step, kv_per_step):
    qts = [q_ref[:, j * d_qk:(j + 1) * d_qk].T for j in range(heads_per_step)]
    kv_of = lambda j: j if kv_per_step > 1 else 0

    def score(c, j):
        jk = kv_of(j)
        return _dot(k_ref[_chunk_rows(c, TK), jk * d_qk:(jk + 1) * d_qk], qts[j])

    def values(c, j):
        return vt_ref[c, kv_of(j) * V_ROWS:(kv_of(j) + 1) * V_ROWS, :]

    def emit(accs):
        for j in range(heads_per_step):
            o = _normalised(accs[j]) * g_ref[:, j * HEAD_W:(j + 1) * HEAD_W]
            o_ref[:, j * HEAD_W:(j + 1) * HEAD_W] = o.astype(BF16)

    accs, overflowed = _stream_softmax(heads_per_step, score, values, N_CHUNKS, TQ_PLAIN)
    emit(accs)

    @pl.when(overflowed)
    def _():
        emit(_exact_softmax(heads_per_step, score, values, 0, N_CHUNKS, TQ_PLAIN))


def _gated_attn(q, k, vt, gate, *, d_qk, q_heads_per_kv, heads_per_step, gate_block0, name):
    b, s, _ = q.shape
    hs = heads_per_step
    kv_per_step = max(1, hs // q_heads_per_kv)
    steps_per_kv = max(1, q_heads_per_kv // hs)
    gate0 = gate_block0 // hs
    return pl.pallas_call(
        functools.partial(_gated_attn_kernel, d_qk=d_qk, heads_per_step=hs, kv_per_step=kv_per_step),
        grid=(b, HEADS // hs, s // TQ_PLAIN),
        in_specs=[
            pl.BlockSpec((None, TQ_PLAIN, hs * d_qk), lambda bb, h, qi: (bb, qi, h)),
            pl.BlockSpec((None, s, kv_per_step * d_qk), lambda bb, h, qi: (bb, 0, h // steps_per_kv)),
            pl.BlockSpec((N_CHUNKS, kv_per_step * V_ROWS, TK), lambda bb, h, qi: (bb, h // steps_per_kv, 0)),
            pl.BlockSpec((None, TQ_PLAIN, hs * HEAD_W), lambda bb, h, qi: (bb, qi, gate0 + h)),
        ],
        out_specs=pl.BlockSpec((None, TQ_PLAIN, hs * HEAD_W), lambda bb, h, qi: (bb, qi, h)),
        out_shape=jax.ShapeDtypeStruct((b, s, HEADS * HEAD_W), BF16),
        compiler_params=pltpu.CompilerParams(
            dimension_semantics=("arbitrary",) * 3, vmem_limit_bytes=VMEM_LIMIT),
        name=name,
    )(q, k, vt, gate)


def _outproj_kernel(oa_ref, ob_ref, w_ref, x_ref, pg_ref, y_ref):
    half = oa_ref.shape[-1]
    m = _dot(oa_ref[...], w_ref[:half, :]) + _dot(ob_ref[...], w_ref[half:, :])
    y_ref[...] = x_ref[...] + _rms(m, pg_ref[...])


def _outproj(oa, ob, w, x2, post_gain, name):
    m = x2.shape[0]
    tm = TM_OUT
    row = lambda i: (i, 0)
    fixed = lambda i: (0, 0)
    return pl.pallas_call(
        _outproj_kernel,
        grid=(m // tm,),
        in_specs=[
            pl.BlockSpec((tm, oa.shape[-1]), row),
            pl.BlockSpec((tm, ob.shape[-1]), row),
            pl.BlockSpec(w.shape, fixed, pipeline_mode=pl.Buffered(1)),
            pl.BlockSpec((tm, D_MODEL), row),
            pl.BlockSpec((1, D_MODEL), fixed),
        ],
        out_specs=pl.BlockSpec((tm, D_MODEL), row),
        out_shape=jax.ShapeDtypeStruct((m, D_MODEL), F32),
        compiler_params=pltpu.CompilerParams(
            dimension_semantics=("arbitrary",), vmem_limit_bytes=VMEM_LIMIT),
        name=name,
    )(oa, ob, w, x2, post_gain)


_O_QKV, _O_GC, _O_QD, _O_KD, _O_GD, _O_COLS = 0, 3072, 4096, 5120, 5632, 6656


def _odd_inproj_kernel(x_ref, pg_ref, w_ref, wvt_ref, qn_ref, kn_ref, cos_ref, sin_ref,
                       qkv_ref, vt_ref, gate_ref, qd_ref, kd_ref):
    h = _rms(x_ref[...], pg_ref[...]).astype(BF16)
    cos = cos_ref[...]
    sin = sin_ref[...]
    scale = HEAD_W ** -0.5 * LOG2E

    def proj(c0, n):
        return _dot(h, w_ref[:, c0:c0 + n])

    for c in range(0, 1024, _N_CHUNK):
        qkv_ref[:, c:c + _N_CHUNK] = (proj(_O_QKV + c, _N_CHUNK) * scale).astype(BF16)
    for c in range(1024, 3072, _N_CHUNK):
        qkv_ref[:, c:c + _N_CHUNK] = proj(_O_QKV + c, _N_CHUNK).astype(BF16)
    _store_vt(vt_ref, _dot_nt(wvt_ref[...], h), D_KV_HEADS)
    for c in range(0, 1024, _N_CHUNK):
        gate_ref[:, c:c + _N_CHUNK] = _silu(proj(_O_GC + c, _N_CHUNK))
        gate_ref[:, 1024 + c:1024 + c + _N_CHUNK] = _silu(proj(_O_GD + c, _N_CHUNK))
    for c in range(0, 1024, _N_CHUNK):
        qd = proj(_O_QD + c, _N_CHUNK)
        for j in range(0, _N_CHUNK, HEAD_W):
            qh = _rope(_rms(qd[:, j:j + HEAD_W], qn_ref[...]), cos, sin) * scale
            qd_ref[:, c + j:c + j + HEAD_W] = qh.astype(BF16)
    kd = proj(_O_KD, 256)
    for j in range(0, 256, HEAD_W):
        kh = _rope(_rms(kd[:, j:j + HEAD_W], kn_ref[...]), cos, sin)
        kd_ref[:, j:j + HEAD_W] = kh.astype(BF16)


def _odd_inproj(x2, pre_gain, w_perm, wvt, q_norm, k_norm, cos_ax, sin_ax):
    m = x2.shape[0]
    tm = TM_PROJ
    pos_blocks = SEQ // tm
    row = lambda i: (i, 0)
    fixed = lambda i: (0, 0)
    pos = lambda i: (i % pos_blocks, 0)
    return pl.pallas_call(
        _odd_inproj_kernel,
        grid=(m // tm,),
        in_specs=[
            pl.BlockSpec((tm, D_MODEL), row),
            pl.BlockSpec((1, D_MODEL), fixed),
            pl.BlockSpec((D_MODEL, _O_COLS), fixed, pipeline_mode=pl.Buffered(1)),
            pl.BlockSpec((256, D_MODEL), fixed, pipeline_mode=pl.Buffered(1)),
            pl.BlockSpec((1, HEAD_W), fixed),
            pl.BlockSpec((1, HEAD_W), fixed),
            pl.BlockSpec((tm, HEAD_W), pos),
            pl.BlockSpec((tm, HEAD_W), pos),
        ],
        out_specs=[
            pl.BlockSpec((tm, 3072), row),
            _vt_out_spec(D_KV_HEADS, tm),
            pl.BlockSpec((tm, 2048), row),
            pl.BlockSpec((tm, 1024), row),
            pl.BlockSpec((tm, 256), row),
        ],
        out_shape=[
            jax.ShapeDtypeStruct((m, 3072), BF16),
            _vt_shape(m, D_KV_HEADS),
            jax.ShapeDtypeStruct((m, 2048), F32),
            jax.ShapeDtypeStruct((m, 1024), BF16),
            jax.ShapeDtypeStruct((m, 256), BF16),
        ],
        compiler_params=pltpu.CompilerParams(
            dimension_semantics=("arbitrary",), vmem_limit_bytes=VMEM_LIMIT),
        name="odd_inproj",
    )(x2, pre_gain, w_perm, wvt, q_norm, k_norm, cos_ax, sin_ax)


_NA_Q = NA_GROUP_ROWS * GRID_W
_NA_K = NA_BAND_ROWS * GRID_W
_NA_GROUPS = SEQ // _NA_Q
_NA_ROWS = SEQ // GRID_W
_NA_LAST_START = _NA_ROWS - NA_BAND_ROWS
_NA_CLASS_GROUPS = (0, 1, _NA_GROUPS - 1)


def _na_band_start(g):
    lo = g * NA_GROUP_ROWS - NA_KH // 2
    if isinstance(g, int):
        return min(max(lo, 0), _NA_LAST_START)
    return jnp.clip(lo, 0, _NA_LAST_START)


def _na_build_bias(rpb_ref, bias_ref):
    vec = rpb_ref[...] * LOG2E
    even_src = pltpu.roll(vec, 128 - (NA_KW - 1), 1)
    odd_src = pltpu.roll(vec, GRID_W - (NA_KW - 1), 1)
    q_idx = lax.broadcasted_iota(jnp.int32, (GRID_W, 128), 0)
    lane = lax.broadcasted_iota(jnp.int32, (GRID_W, 128), 1)
    w_idx = lane & (GRID_W - 1)
    c0 = jnp.clip(q_idx - NA_KW // 2, 0, GRID_W - NA_KW)
    col_in = (w_idx >= c0) & (w_idx < c0 + NA_KW)
    neg = jnp.full((GRID_W, 128), NEG_INF, F32)
    cache = {}
    for cls, g in enumerate(_NA_CLASS_GROUPS):
        start = _na_band_start(g)
        for qr in range(NA_GROUP_ROWS):
            rq = g * NA_GROUP_ROWS + qr
            r0 = min(max(rq - NA_KH // 2, 0), _NA_ROWS - NA_KH)
            for kp in range(NA_BAND_ROWS // 2):
                halves = []
                for rk in (start + 2 * kp, start + 2 * kp + 1):
                    halves.append(rk - rq + NA_KH - 1 if r0 <= rk < r0 + NA_KH else None)
                key = tuple(halves)
                if key not in cache:
                    if halves[0] is None and halves[1] is None:
                        cache[key] = neg
                    else:
                        src = jnp.zeros((1, 128), F32)
                        mask = None
                        if halves[0] is not None:
                            src = src + even_src[halves[0]:halves[0] + 1, :]
                            mask = lane < GRID_W
                        if halves[1] is not None:
                            src = src + odd_src[halves[1]:halves[1] + 1, :]
                            mask = (lane >= GRID_W) if mask is None else None
                        tile = pltpu.roll(jnp.broadcast_to(src, (GRID_W, 128)), 0, 1, stride=1, stride_axis=0)
                        keep = col_in if mask is None else (col_in & mask)
                        cache[key] = jnp.where(keep, tile, neg)
                bias_ref[cls, qr * GRID_W:(qr + 1) * GRID_W, kp * 128:(kp + 1) * 128] = cache[key]


def _na_kernel(q_ref, k_ref, v_ref, rpb_ref, g_ref, o_ref, bias_ref):
    g = pl.program_id(2)

    @pl.when(g == 0)
    def _():
        for j in range(NA_HEADS_PER_STEP):
            _na_build_bias(rpb_ref.at[j], bias_ref.at[j])

    cls = jnp.where(g == 0, 0, jnp.where(g == _NA_GROUPS - 1, 2, 1))
    start = pl.multiple_of(_na_band_start(g) * GRID_W, GRID_W)
    own = pl.multiple_of(g * _NA_Q, _NA_Q)
    heads = [slice(j * HEAD_W, (j + 1) * HEAD_W) for j in range(NA_HEADS_PER_STEP)]
    ones = jnp.ones((_NA_K, HEAD_W), BF16)

    def scores(j):
        return _dot_nt(q_ref[:, heads[j]], k_ref[pl.ds(start, _NA_K), heads[j]]) + bias_ref[j, cls]

    def attend(j, s, ref):
        p = jnp.exp2(s - ref).astype(BF16)
        return _dot(p, jnp.concatenate([v_ref[pl.ds(start, _NA_K), heads[j]], ones], axis=1))

    def emit(j, ol):
        o = ol[:, :HEAD_W] * (1.0 / ol[:, HEAD_W:HEAD_W + 1])
        o_ref[:, heads[j]] = (o * g_ref[:, heads[j]]).astype(BF16)

    worst = None
    for j in range(NA_HEADS_PER_STEP):
        q_own = q_ref[:, heads[j]].astype(F32) * k_ref[pl.ds(own, _NA_Q), heads[j]].astype(F32)
        self_score = jnp.sum(q_own, axis=1, keepdims=True) + rpb_ref[j, NA_KH - 1:NA_KH, NA_KW - 1:NA_KW] * LOG2E
        ol = attend(j, scores(j), self_score)
        emit(j, ol)
        finite = jnp.min(jnp.where(jnp.isfinite(ol), 1.0, 0.0))
        worst = finite if worst is None else jnp.minimum(worst, finite)

    @pl.when(worst < 0.5)
    def _():
        for j in range(NA_HEADS_PER_STEP):
            s = scores(j)
            emit(j, attend(j, s, jnp.max(s, axis=1, keepdims=True)))


def _na_attn(qkv, gate, rpb_pad):
    b, s, _ = qkv.shape
    hs = NA_HEADS_PER_STEP
    steps = HEADS // hs
    return pl.pallas_call(
        _na_kernel,
        grid=(b, steps, _NA_GROUPS),
        in_specs=[
            pl.BlockSpec((None, _NA_Q, hs * HEAD_W), lambda bb, h, g: (bb, g, h)),
            pl.BlockSpec((None, s, hs * HEAD_W), lambda bb, h, g: (bb, 0, steps + h)),
            pl.BlockSpec((None, s, hs * HEAD_W), lambda bb, h, g: (bb, 0, 2 * steps + h)),
            pl.BlockSpec((hs, 16, 128), lambda bb, h, g: (h, 0, 0)),
            pl.BlockSpec((None, _NA_Q, hs * HEAD_W), lambda bb, h, g: (bb, g, h)),
        ],
        out_specs=pl.BlockSpec((None, _NA_Q, hs * HEAD_W), lambda bb, h, g: (bb, g, h)),
        out_shape=jax.ShapeDtypeStruct((b, s, HEADS * HEAD_W), BF16),
        scratch_shapes=[pltpu.VMEM((hs, 3, _NA_Q, _NA_K), F32)],
        compiler_params=pltpu.CompilerParams(
            dimension_semantics=("arbitrary",) * 3, vmem_limit_bytes=VMEM_LIMIT),
        name="na_attn",
    )(qkv, qkv, qkv, rpb_pad, gate)


def _rope_seg_tables(pos):
    inv = ROPE_THETA ** (-np.arange(ROPE_HALF, dtype=np.float64) / ROPE_HALF)
    ang = pos.astype(np.float64)[:, None] * inv[None, :]
    cos, sin = np.cos(ang), np.sin(ang)
    return np.concatenate([cos, cos], axis=1), np.concatenate([-sin, sin], axis=1)


def _rope_tables():
    t = np.arange(SEQ)
    cos_t, sin_t = _rope_seg_tables(t)
    ones, zeros = np.ones((SEQ, ROPE_SEG)), np.zeros((SEQ, ROPE_SEG))
    cos_k = np.concatenate([cos_t, ones], axis=1)
    sin_k = np.concatenate([sin_t, zeros], axis=1)
    cos_q = np.concatenate([ones, ones, cos_t, ones], axis=1)
    sin_q = np.concatenate([zeros, zeros, sin_t, zeros], axis=1)
    cos_r, sin_r = _rope_seg_tables(t // GRID_W)
    cos_c, sin_c = _rope_seg_tables(t % GRID_W)
    cos_ax = np.concatenate([cos_r, cos_c], axis=1)
    sin_ax = np.concatenate([sin_r, sin_c], axis=1)
    as_f32 = lambda a: jnp.asarray(a, F32)
    return tuple(map(as_f32, (cos_k, sin_k, cos_q, sin_q, cos_ax, sin_ax)))


def _transpose_cols_kernel(w_ref, o_ref):
    o_ref[...] = w_ref[...].T.astype(BF16)


def _transpose_cols(w, col0, width):
    k = w.shape[0]
    rows = 256
    assert col0 % width == 0 and k % rows == 0
    return pl.pallas_call(
        _transpose_cols_kernel,
        grid=(k // rows,),
        in_specs=[pl.BlockSpec((rows, width), lambda i: (i, col0 // width))],
        out_specs=pl.BlockSpec((width, rows), lambda i: (0, i)),
        out_shape=jax.ShapeDtypeStruct((width, k), BF16),
        name="transpose_cols",
    )(w)


def _transposed_cast(wt):
    n, k = wt.shape
    rows = 256
    return pl.pallas_call(
        _transpose_cols_kernel,
        grid=(pl.cdiv(n, rows),),
        in_specs=[pl.BlockSpec((rows, k), lambda i: (i, 0))],
        out_specs=pl.BlockSpec((k, rows), lambda i: (0, i)),
        out_shape=jax.ShapeDtypeStruct((k, n), BF16),
        name="transposed_cast",
    )(wt)


def _even_weights(w_in, w_uq, w_ukv):
    w_t = jnp.swapaxes(w_in, 0, 1)
    w_bf = _transposed_cast(w_t)
    va_t = w_t[2048:3072].astype(BF16)
    wq = w_uq.reshape(B_Q_RANK, HEADS, B_NOPE + B_ROPE)
    wq = jnp.pad(wq, ((0, 0), (0, 0), (0, B_QK_PAD - B_NOPE - B_ROPE))).reshape(B_Q_RANK, HEADS * B_QK_PAD)
    wkv = w_ukv.reshape(B_KV_RANK, HEADS, B_NOPE + HEAD_W)
    wk = wkv[:, :, :B_NOPE].reshape(B_KV_RANK, HEADS * B_NOPE)
    wv = wkv[:, :, B_NOPE:].reshape(B_KV_RANK, HEADS * HEAD_W)
    return (w_bf, va_t, wq.astype(BF16), wk.astype(BF16), wv.T.astype(BF16))


def _odd_weights(w_in):
    return w_in.astype(BF16), _transpose_cols(w_in, 5376, 256)


def _even_layer(x2, batch, layer, pre_gain, post_gain, w_in, w_out, lq1, lk1, lq2, lk2, subln,
                q_norm, w_uq, kv_norm, w_ukv, tables):
    cos_k, sin_k, cos_q, sin_q, _, _ = tables
    lambda_init = 0.8 - 0.6 * math.exp(-0.3 * layer)
    w_perm, wvat, wq, wk, wvbt = _even_weights(w_in, w_uq, w_ukv)
    qk, vat, gate, cq, ckv, kpe = _even_inproj(
        x2, pre_gain[None], w_perm, wvat, q_norm[None], kv_norm[None], cos_k, sin_k)
    as3 = lambda a: a.reshape(batch, SEQ, a.shape[-1])
    slopes = jnp.asarray(2.0 ** (-8.0 * np.arange(1, HEADS + 1) / HEADS), F32)
    qaug, kaug = _alibi_aug_tables()
    oa = _diff_attn(slopes, as3(qk), vat, as3(gate), qaug, kaug, lq1[None], lk1[None], lq2[None],
                    lk2[None], subln[None], lambda_init)
    q_b, k_b, vbt = _mla_up(cq, ckv, kpe, wq, wk, wvbt, cos_q, sin_q)
    ob = _gated_attn(as3(q_b), as3(k_b), vbt, as3(gate), d_qk=B_QK_PAD, q_heads_per_kv=1,
                     heads_per_step=2, gate_block0=HEADS, name="latent_attn")
    m = x2.shape[0]
    return _outproj(oa.reshape(m, -1), ob.reshape(m, -1), w_out.astype(BF16), x2, post_gain[None],
                    "even_outproj")


def _odd_layer(x2, batch, pre_gain, post_gain, w_in, w_out, rpb, q_norm, k_norm, tables):
    cos_ax, sin_ax = tables[4], tables[5]
    w_perm, wvdt = _odd_weights(w_in)
    qkv, vdt, gate, qd, kd = _odd_inproj(
        x2, pre_gain[None], w_perm, wvdt, q_norm[None], k_norm[None], cos_ax, sin_ax)
    as3 = lambda a: a.reshape(batch, SEQ, a.shape[-1])
    rpb_pad = jnp.pad(rpb, ((0, 0), (0, 16 - rpb.shape[1]), (0, 128 - rpb.shape[2])))
    oc = _na_attn(as3(qkv), as3(gate), rpb_pad)
    od = _gated_attn(as3(qd), as3(kd), vdt, as3(gate), d_qk=HEAD_W,
                     q_heads_per_kv=HEADS // D_KV_HEADS, heads_per_step=2, gate_block0=HEADS,
                     name="gqa_attn")
    m = x2.shape[0]
    return _outproj(oc.reshape(m, -1), od.reshape(m, -1), w_out.astype(BF16), x2, post_gain[None],
                    "odd_outproj")


def kernel(x, pre_norm, post_norm, even_w_in, even_w_out, diff_lambda_q1, diff_lambda_k1,
           diff_lambda_q2, diff_lambda_k2, diff_subln, mla_q_norm, mla_w_uq, mla_kv_norm, mla_w_ukv,
           odd_w_in, odd_w_out, na_rpb, gqa_q_norm, gqa_k_norm):
    batch, seq, d_model = x.shape
    assert (seq, d_model) == (SEQ, D_MODEL)
    depth = pre_norm.shape[0]
    tables = _rope_tables()
    x2 = x.reshape(batch * seq, d_model)
    for layer in range(depth):
        i = layer // 2
        if layer % 2 == 0:
            x2 = _even_layer(x2, batch, layer, pre_norm[layer], post_norm[layer], even_w_in[i],
                             even_w_out[i], diff_lambda_q1[i], diff_lambda_k1[i], diff_lambda_q2[i],
                             diff_lambda_k2[i], diff_subln[i], mla_q_norm[i], mla_w_uq[i],
                             mla_kv_norm[i], mla_w_ukv[i], tables)
        else:
            x2 = _odd_layer(x2, batch, pre_norm[layer], post_norm[layer], odd_w_in[i], odd_w_out[i],
                            na_rpb[i], gqa_q_norm[i], gqa_k_norm[i], tables)
    return x2.reshape(batch, seq, d_model)
```

```python
import functools
import math

import numpy as np
import jax
import jax.numpy as jnp
from jax import lax
from jax.experimental import pallas as pl
from jax.experimental.pallas import tpu as pltpu

F32 = jnp.float32
BF16 = jnp.bfloat16

D_MODEL = 2048
SEQ = 4096
GRID_W = 64
NORM_EPS = 1e-6
ROPE_THETA = 10000.0
NEG_INF = -1e30
LOG2E = math.log2(math.e)

HEADS = 8
HEAD_W = 128
V_ROWS = HEAD_W + 16
A_DQK = 64
B_NOPE = 128
B_ROPE = 64
B_Q_RANK = 768
B_KV_RANK = 512
B_QK_PAD = 256
D_KV_HEADS = 2
NA_KH = 8
NA_KW = 16
NA_GROUP_ROWS = 4
NA_BAND_ROWS = 12
NA_HEADS_PER_STEP = 4

ROPE_SEG = 64
ROPE_HALF = 32

VMEM_LIMIT = 60 * 1024 * 1024

TM_PROJ = 256
TM_OUT = 512
TK = 512
N_CHUNKS = SEQ // TK
TQ_DIFF = 2048
_DIAG_CHUNKS = TQ_DIFF // TK
TQ_PLAIN = 2048


def _rms(xf, gain):
    ms = jnp.mean(xf * xf, axis=-1, keepdims=True)
    return xf * lax.rsqrt(ms + NORM_EPS) * gain


def _silu(g):
    return g * (1.0 / (1.0 + jnp.exp(-g)))


def _rope(x, cos, sin_signed):
    w = x.shape[-1]
    lane = lax.broadcasted_iota(jnp.int32, x.shape, 1)
    first_half = (lane & (ROPE_SEG - 1)) < ROPE_HALF
    partner = jnp.where(first_half, pltpu.roll(x, w - ROPE_HALF, 1), pltpu.roll(x, ROPE_HALF, 1))
    return x * cos + partner * sin_signed


def _dot(a, b):
    return jnp.dot(a, b, preferred_element_type=F32)


def _dot_nt(a, b):
    return lax.dot_general(a, b, (((1,), (1,)), ((), ())), preferred_element_type=F32)


def _self_scores(q, k_own):
    ones = jnp.ones((8, q.shape[1]), BF16)
    return _dot_nt(ones, q * k_own)[:1, :]


def _stream_softmax(refs, score_fn, vt_fn, n_steps, tq):
    n_streams = len(refs)
    accs = [jnp.zeros((V_ROWS, tq), F32) for _ in refs]
    cur = [score_fn(0, j) for j in range(n_streams)]
    for step in range(n_steps):
        nxt = [score_fn(step + 1, j) for j in range(n_streams)] if step + 1 < n_steps else None
        for j in range(n_streams):
            accs[j] = accs[j] + _dot(vt_fn(step, j), jnp.exp2(cur[j] - refs[j]).astype(BF16))
        cur = nxt
    finite = [jnp.min(jnp.where(jnp.isfinite(a), 1.0, 0.0)) for a in accs]
    return accs, functools.reduce(jnp.minimum, finite) < 0.5


def _exact_step(carry, st, vt):
    m, acc = carry
    m_new = jnp.maximum(m, jnp.max(st, axis=0, keepdims=True))
    p = jnp.exp2(st - m_new).astype(BF16)
    return m_new, jnp.exp2(m - m_new) * acc + _dot(vt, p)


def _exact_softmax(n_streams, score_fn, vt_fn, n_static, n_steps, tq):
    carries = [(jnp.full((1, tq), NEG_INF, F32), jnp.zeros((V_ROWS, tq), F32)) for _ in range(n_streams)]
    for step in range(n_static):
        carries = [_exact_step(carries[j], score_fn(step, j), vt_fn(step, j)) for j in range(n_streams)]

    def body(step, carries):
        return tuple(_exact_step(carries[j], score_fn(step, j), vt_fn(step, j)) for j in range(n_streams))

    carries = lax.fori_loop(n_static, n_steps, body, tuple(carries))
    return [c[1] for c in carries]


def _normalised(acc):
    l = acc[HEAD_W:HEAD_W + 1, :]
    return (acc[:HEAD_W, :] * (1.0 / l)).T


def _chunk_rows(c, n):
    if isinstance(c, int):
        return slice(c * n, (c + 1) * n)
    return pl.ds(pl.multiple_of(c * n, n), n)


def _store_vt(vt_ref, vt, n_heads):
    ones = jnp.ones((V_ROWS - HEAD_W, vt.shape[1]), BF16)
    for h in range(n_heads):
        vt_ref[h * V_ROWS:h * V_ROWS + HEAD_W, :] = vt[h * HEAD_W:(h + 1) * HEAD_W, :].astype(BF16)
        vt_ref[h * V_ROWS + HEAD_W:(h + 1) * V_ROWS, :] = ones


_E_QK, _E_GA, _E_CQ, _E_CKV, _E_KR, _E_COLS = 0, 3072, 4096, 4864, 5376, 6464
_N_CHUNK = 512


def _even_inproj_kernel(x_ref, pg_ref, w_ref, wvt_ref, qn_ref, kvn_ref, cos_ref, sin_ref,
                        qk_ref, vt_ref, gate_ref, cq_ref, ckv_ref, kpe_ref):
    h = _rms(x_ref[...], pg_ref[...]).astype(BF16)

    def proj(c0, n):
        return _dot(h, w_ref[:, c0:c0 + n])

    tail = proj(_E_KR, _E_COLS - _E_KR)
    gate_ref[:, 1024:2048] = _silu(tail[:, B_ROPE:])
    kr = tail[:, :128]
    lane = lax.broadcasted_iota(jnp.int32, kr.shape, 1)
    kpe = jnp.where(lane < B_ROPE, _rope(kr, cos_ref[...], sin_ref[...]), 0.0)
    kpe_ref[...] = kpe.astype(BF16)
    cq_ref[...] = _rms(proj(_E_CQ, B_Q_RANK), qn_ref[...]).astype(BF16)
    ckv_ref[...] = _rms(proj(_E_CKV, B_KV_RANK), kvn_ref[...]).astype(BF16)
    for c in range(0, 1024, _N_CHUNK):
        gate_ref[:, c:c + _N_CHUNK] = _silu(proj(_E_GA + c, _N_CHUNK))
    _store_vt(vt_ref, _dot_nt(wvt_ref[...], h), HEADS)
    q_scale = (A_DQK ** -0.5) * LOG2E
    for c in range(0, 1024, _N_CHUNK):
        qk_ref[:, c:c + _N_CHUNK] = (proj(_E_QK + c, _N_CHUNK) * q_scale).astype(BF16)
    for c in range(1024, 2048, _N_CHUNK):
        qk_ref[:, c:c + _N_CHUNK] = proj(_E_QK + c, _N_CHUNK).astype(BF16)


def _vt_out_spec(n_heads, tm):
    per_chunk = TK // tm
    return pl.BlockSpec((None, n_heads * V_ROWS, tm), lambda i: (i // per_chunk, 0, i % per_chunk))


def _vt_shape(m, n_heads):
    return jax.ShapeDtypeStruct((m // TK, n_heads * V_ROWS, TK), BF16)


def _even_inproj(x2, pre_gain, w_perm, wvt, q_norm, kv_norm, cos_k, sin_k):
    m = x2.shape[0]
    tm = TM_PROJ
    pos_blocks = SEQ // tm
    row = lambda i: (i, 0)
    fixed = lambda i: (0, 0)
    pos = lambda i: (i % pos_blocks, 0)
    return pl.pallas_call(
        _even_inproj_kernel,
        grid=(m // tm,),
        in_specs=[
            pl.BlockSpec((tm, D_MODEL), row),
            pl.BlockSpec((1, D_MODEL), fixed),
            pl.BlockSpec((D_MODEL, _E_COLS), fixed, pipeline_mode=pl.Buffered(1)),
            pl.BlockSpec((1024, D_MODEL), fixed, pipeline_mode=pl.Buffered(1)),
            pl.BlockSpec((1, B_Q_RANK), fixed),
            pl.BlockSpec((1, B_KV_RANK), fixed),
            pl.BlockSpec((tm, 128), pos),
            pl.BlockSpec((tm, 128), pos),
        ],
        out_specs=[
            pl.BlockSpec((tm, 2048), row),
            _vt_out_spec(HEADS, tm),
            pl.BlockSpec((tm, 2048), row),
            pl.BlockSpec((tm, B_Q_RANK), row),
            pl.BlockSpec((tm, B_KV_RANK), row),
            pl.BlockSpec((tm, 128), row),
        ],
        out_shape=[
            jax.ShapeDtypeStruct((m, 2048), BF16),
            _vt_shape(m, HEADS),
            jax.ShapeDtypeStruct((m, 2048), F32),
            jax.ShapeDtypeStruct((m, B_Q_RANK), BF16),
            jax.ShapeDtypeStruct((m, B_KV_RANK), BF16),
            jax.ShapeDtypeStruct((m, 128), BF16),
        ],
        compiler_params=pltpu.CompilerParams(
            dimension_semantics=("arbitrary",), vmem_limit_bytes=VMEM_LIMIT),
        name="even_inproj",
    )(x2, pre_gain, w_perm, wvt, q_norm, kv_norm, cos_k, sin_k)


def _log2e_pieces():
    pieces, rest = [], np.float64(LOG2E)
    for _ in range(3):
        p = np.float64(np.asarray(rest).astype(BF16))
        pieces.append(p)
        rest = rest - p
    return pieces


def _alibi_aug_tables():
    pos = np.arange(SEQ)
    hi, lo = (pos // 64) * 64.0, (pos % 64) * 1.0
    c = _log2e_pieces()
    qaug = np.zeros((SEQ, HEAD_W))
    kaug = np.zeros((SEQ, HEAD_W))
    for p in range(3):
        qaug[:, p], kaug[:, p] = c[p], hi
        qaug[:, 3 + p], kaug[:, 3 + p] = c[p], lo
        qaug[:, 6 + p], kaug[:, 6 + p] = -hi, c[p]
        qaug[:, 9 + p], kaug[:, 9 + p] = -lo, c[p]
    return jnp.asarray(qaug, F32), jnp.asarray(kaug, BF16)


def _diff_attn_kernel(slopes_ref, q_ref, k_ref, vt_ref, g_ref, qaug_ref, kaug_ref,
                      lq1_ref, lk1_ref, lq2_ref, lk2_ref, subln_ref, o_ref, kcat_ref, dbias_ref,
                      *, lambda_init):
    head = pl.program_id(1)
    qi = pl.program_id(2)
    slope = slopes_ref[head]

    @pl.when(qi == 0)
    def _():
        k = k_ref[...]
        kaug = kaug_ref[...]
        kcat_ref[0, :, :HEAD_W] = k
        kcat_ref[0, :, HEAD_W:] = kaug
        kcat_ref[1, :, :HEAD_W] = k
        kcat_ref[1, :, HEAD_W:] = -kaug
        key = lax.broadcasted_iota(jnp.int32, (TK, TK), 0)
        qry = lax.broadcasted_iota(jnp.int32, (TK, TK), 1)
        dbias_ref[...] = jnp.abs(qry - key).astype(F32) * (-slope * LOG2E)

    q = q_ref[...]
    lane = lax.broadcasted_iota(jnp.int32, q.shape, 1)
    zero = jnp.zeros_like(q)
    q_maps = (jnp.where(lane < A_DQK, q, zero), jnp.where(lane >= A_DQK, q, zero))
    qaug = (qaug_ref[...] * slope).astype(BF16)
    q_diag = [jnp.concatenate([qm, zero], axis=1) for qm in q_maps]
    q_off = [jnp.concatenate([qm, qaug], axis=1) for qm in q_maps]
    q_rev = [jnp.concatenate([qm, -qaug], axis=1) for qm in q_maps]

    k_own = k_ref[pl.ds(pl.multiple_of(qi * TQ_DIFF, TQ_DIFF), TQ_DIFF), :]
    refs = [_self_scores(qm, k_own) for qm in q_maps]

    def chunk_of(r):
        first = qi * _DIAG_CHUNKS
        wrapped = (first + r >= N_CHUNKS).astype(jnp.int32)
        return first + r - wrapped * N_CHUNKS, wrapped

    def score(r, mi):
        c, wrapped = chunk_of(r)
        rows = pl.ds(pl.multiple_of(c * TK, TK), TK)
        if not (isinstance(r, int) and r < _DIAG_CHUNKS):
            return _dot_nt(kcat_ref[1 - wrapped, rows, :], q_off[mi])
        kc = kcat_ref[0, rows, :]
        blocks = []
        for u in range(_DIAG_CHUNKS):
            sub = slice(u * TK, (u + 1) * TK)
            if u == r:
                blocks.append(_dot_nt(kc, q_diag[mi][sub]) + dbias_ref[...])
            else:
                blocks.append(_dot_nt(kc, (q_off if u > r else q_rev)[mi][sub]))
        return blocks[0] if len(blocks) == 1 else jnp.concatenate(blocks, axis=1)

    def values(r, mi):
        return vt_ref[chunk_of(r)[0]]

    lam = (jnp.exp(jnp.sum(lq1_ref[...] * lk1_ref[...], axis=-1, keepdims=True))
           - jnp.exp(jnp.sum(lq2_ref[...] * lk2_ref[...], axis=-1, keepdims=True))
           + lambda_init)

    def emit(accs):
        o = _normalised(accs[0]) - lam * _normalised(accs[1])
        o = _rms(o, subln_ref[...]) * (1.0 - lambda_init)
        o_ref[...] = (o * g_ref[...]).astype(BF16)

    accs, overflowed = _stream_softmax(refs, score, values, N_CHUNKS, TQ_DIFF)
    emit(accs)

    @pl.when(overflowed)
    def _():
        emit(_exact_softmax(2, score, values, _DIAG_CHUNKS, N_CHUNKS, TQ_DIFF))


def _diff_attn(slopes, qk, vt, gate, qaug, kaug, lq1, lk1, lq2, lk2, subln, lambda_init):
    b, s, _ = qk.shape
    vec = lambda bb, h, qi: (0, 0)
    return pl.pallas_call(
        functools.partial(_diff_attn_kernel, lambda_init=lambda_init),
        grid=(b, HEADS, s // TQ_DIFF),
        in_specs=[
            pl.BlockSpec(memory_space=pltpu.SMEM),
            pl.BlockSpec((None, TQ_DIFF, HEAD_W), lambda bb, h, qi: (bb, qi, h)),
            pl.BlockSpec((None, s, HEAD_W), lambda bb, h, qi: (bb, 0, HEADS + h)),
            pl.BlockSpec((N_CHUNKS, V_ROWS, TK), lambda bb, h, qi: (bb, h, 0)),
            pl.BlockSpec((None, TQ_DIFF, HEAD_W), lambda bb, h, qi: (bb, qi, h)),
            pl.BlockSpec((TQ_DIFF, HEAD_W), lambda bb, h, qi: (qi, 0)),
            pl.BlockSpec((s, HEAD_W), vec),
            pl.BlockSpec((1, A_DQK), vec),
            pl.BlockSpec((1, A_DQK), vec),
            pl.BlockSpec((1, A_DQK), vec),
            pl.BlockSpec((1, A_DQK), vec),
            pl.BlockSpec((1, HEAD_W), vec),
        ],
        out_specs=pl.BlockSpec((None, TQ_DIFF, HEAD_W), lambda bb, h, qi: (bb, qi, h)),
        out_shape=jax.ShapeDtypeStruct((b, s, HEADS * HEAD_W), BF16),
        scratch_shapes=[pltpu.VMEM((2, s, 2 * HEAD_W), BF16),
                        pltpu.VMEM((TK, TK), F32)],
        compiler_params=pltpu.CompilerParams(
            dimension_semantics=("arbitrary",) * 3, vmem_limit_bytes=VMEM_LIMIT),
        name="diff_attn",
    )(slopes, qk, qk, vt, gate, qaug, kaug, lq1, lk1, lq2, lk2, subln)


def _mla_up_kernel(cq_ref, ckv_ref, kpe_ref, wq_ref, wk_ref, wvt_ref, cos_ref, sin_ref,
                   q_ref, k_ref, vt_ref):
    cq = cq_ref[...]
    ckv = ckv_ref[...]
    kpe = kpe_ref[...]
    cos = cos_ref[...]
    sin = sin_ref[...]
    scale = (B_NOPE + B_ROPE) ** -0.5 * LOG2E
    for h in range(HEADS):
        qh = _dot(cq, wq_ref[:, h * B_QK_PAD:(h + 1) * B_QK_PAD])
        q_ref[:, h * B_QK_PAD:(h + 1) * B_QK_PAD] = (_rope(qh, cos, sin) * scale).astype(BF16)
        k_ref[:, h * B_QK_PAD + B_NOPE:(h + 1) * B_QK_PAD] = kpe
    for h in range(0, HEADS, 2):
        kk = _dot(ckv, wk_ref[:, h * B_NOPE:(h + 2) * B_NOPE]).astype(BF16)
        k_ref[:, h * B_QK_PAD:h * B_QK_PAD + B_NOPE] = kk[:, :B_NOPE]
        k_ref[:, (h + 1) * B_QK_PAD:(h + 1) * B_QK_PAD + B_NOPE] = kk[:, B_NOPE:]
    _store_vt(vt_ref, _dot_nt(wvt_ref[...], ckv), HEADS)


def _mla_up(cq, ckv, kpe, wq, wk, wvt, cos_q, sin_q):
    m = cq.shape[0]
    tm = TK
    pos_blocks = SEQ // tm
    row = lambda i: (i, 0)
    fixed = lambda i: (0, 0)
    pos = lambda i: (i % pos_blocks, 0)
    return pl.pallas_call(
        _mla_up_kernel,
        grid=(m // tm,),
        in_specs=[
            pl.BlockSpec((tm, B_Q_RANK), row),
            pl.BlockSpec((tm, B_KV_RANK), row),
            pl.BlockSpec((tm, 128), row),
            pl.BlockSpec(wq.shape, fixed),
            pl.BlockSpec(wk.shape, fixed),
            pl.BlockSpec(wvt.shape, fixed),
            pl.BlockSpec((tm, B_QK_PAD), pos),
            pl.BlockSpec((tm, B_QK_PAD), pos),
        ],
        out_specs=[
            pl.BlockSpec((tm, HEADS * B_QK_PAD), row),
            pl.BlockSpec((tm, HEADS * B_QK_PAD), row),
            _vt_out_spec(HEADS, tm),
        ],
        out_shape=[
            jax.ShapeDtypeStruct((m, HEADS * B_QK_PAD), BF16),
            jax.ShapeDtypeStruct((m, HEADS * B_QK_PAD), BF16),
            _vt_shape(m, HEADS),
        ],
        compiler_params=pltpu.CompilerParams(
            dimension_semantics=("arbitrary",), vmem_limit_bytes=VMEM_LIMIT),
        name="mla_up",
    )(cq, ckv, kpe, wq, wk, wvt, cos_q, sin_q)


def _gated_attn_kernel(q_ref, k_ref, vt_ref, g_ref, o_ref, qt_ref, *, d_qk, heads_per_step, kv_per_step):
    for j in range(heads_per_step):
        qt_ref[j] = q_ref[:, j * d_qk:(j + 1) * d_qk].T
    kv_of = lambda j: j if kv_per_step > 1 else 0

    def score(c, j):
        jk = kv_of(j)
        return _dot(k_ref[_chunk_rows(c, TK), jk * d_qk:(jk + 1) * d_qk], qt_ref[j])

    def values(c, j):
        return vt_ref[c, kv_of(j) * V_ROWS:(kv_of(j) + 1) * V_ROWS, :]

    def emit(accs):
        for j in range(heads_per_step):
            o = _normalised(accs[j]) * g_ref[:, j * HEAD_W:(j + 1) * HEAD_W]
            o_ref[:, j * HEAD_W:(j + 1) * HEAD_W] = o.astype(BF16)

    own = pl.ds(pl.multiple_of(pl.program_id(2) * TQ_PLAIN, TQ_PLAIN), TQ_PLAIN)
    refs = [_self_scores(q_ref[:, j * d_qk:(j + 1) * d_qk],
                         k_ref[own, kv_of(j) * d_qk:(kv_of(j) + 1) * d_qk]) for j in range(heads_per_step)]

    accs, overflowed = _stream_softmax(refs, score, values, N_CHUNKS, TQ_PLAIN)
    emit(accs)

    @pl.when(overflowed)
    def _():
        emit(_exact_softmax(heads_per_step, score, values, 0, N_CHUNKS, TQ_PLAIN))


def _gated_attn(q, k, vt, gate, *, d_qk, q_heads_per_kv, heads_per_step, gate_block0, name):
    b, s, _ = q.shape
    hs = heads_per_step
    kv_per_step = max(1, hs // q_heads_per_kv)
    steps_per_kv = max(1, q_heads_per_kv // hs)
    gate0 = gate_block0 // hs
    return pl.pallas_call(
        functools.partial(_gated_attn_kernel, d_qk=d_qk, heads_per_step=hs, kv_per_step=kv_per_step),
        grid=(b, HEADS // hs, s // TQ_PLAIN),
        in_specs=[
            pl.BlockSpec((None, TQ_PLAIN, hs * d_qk), lambda bb, h, qi: (bb, qi, h)),
            pl.BlockSpec((None, s, kv_per_step * d_qk), lambda bb, h, qi: (bb, 0, h // steps_per_kv)),
            pl.BlockSpec((N_CHUNKS, kv_per_step * V_ROWS, TK), lambda bb, h, qi: (bb, h // steps_per_kv, 0)),
            pl.BlockSpec((None, TQ_PLAIN, hs * HEAD_W), lambda bb, h, qi: (bb, qi, gate0 + h)),
        ],
        out_specs=pl.BlockSpec((None, TQ_PLAIN, hs * HEAD_W), lambda bb, h, qi: (bb, qi, h)),
        out_shape=jax.ShapeDtypeStruct((b, s, HEADS * HEAD_W), BF16),
        scratch_shapes=[pltpu.VMEM((hs, d_qk, TQ_PLAIN), BF16)],
        compiler_params=pltpu.CompilerParams(
            dimension_semantics=("arbitrary",) * 3, vmem_limit_bytes=VMEM_LIMIT),
        name=name,
    )(q, k, vt, gate)


def _outproj_kernel(oa_ref, ob_ref, w_ref, x_ref, pg_ref, y_ref):
    half = oa_ref.shape[-1]
    m = _dot(oa_ref[...], w_ref[:half, :]) + _dot(ob_ref[...], w_ref[half:, :])
    y_ref[...] = x_ref[...] + _rms(m, pg_ref[...])


def _outproj(oa, ob, w, x2, post_gain, name):
    m = x2.shape[0]
    tm = TM_OUT
    row = lambda i: (i, 0)
    fixed = lambda i: (0, 0)
    return pl.pallas_call(
        _outproj_kernel,
        grid=(m // tm,),
        in_specs=[
            pl.BlockSpec((tm, oa.shape[-1]), row),
            pl.BlockSpec((tm, ob.shape[-1]), row),
            pl.BlockSpec(w.shape, fixed, pipeline_mode=pl.Buffered(1)),
            pl.BlockSpec((tm, D_MODEL), row),
            pl.BlockSpec((1, D_MODEL), fixed),
        ],
        out_specs=pl.BlockSpec((tm, D_MODEL), row),
        out_shape=jax.ShapeDtypeStruct((m, D_MODEL), F32),
        compiler_params=pltpu.CompilerParams(
            dimension_semantics=("arbitrary",), vmem_limit_bytes=VMEM_LIMIT),
        name=name,
    )(oa, ob, w, x2, post_gain)


_O_QKV, _O_GC, _O_QD, _O_KD, _O_GD, _O_COLS = 0, 3072, 4096, 5120, 5632, 6656


def _odd_inproj_kernel(x_ref, pg_ref, w_ref, wvt_ref, qn_ref, kn_ref, cos_ref, sin_ref,
                       qkv_ref, vt_ref, gate_ref, qd_ref, kd_ref):
    h = _rms(x_ref[...], pg_ref[...]).astype(BF16)
    cos = cos_ref[...]
    sin = sin_ref[...]
    scale = HEAD_W ** -0.5 * LOG2E

    def proj(c0, n):
        return _dot(h, w_ref[:, c0:c0 + n])

    for c in range(0, 1024, _N_CHUNK):
        qd = proj(_O_QD + c, _N_CHUNK)
        for j in range(0, _N_CHUNK, HEAD_W):
            qh = _rope(_rms(qd[:, j:j + HEAD_W], qn_ref[...]), cos, sin) * scale
            qd_ref[:, c + j:c + j + HEAD_W] = qh.astype(BF16)
    kd = proj(_O_KD, 256)
    for j in range(0, 256, HEAD_W):
        kh = _rope(_rms(kd[:, j:j + HEAD_W], kn_ref[...]), cos, sin)
        kd_ref[:, j:j + HEAD_W] = kh.astype(BF16)
    for c in range(0, 1024, _N_CHUNK):
        gate_ref[:, c:c + _N_CHUNK] = _silu(proj(_O_GC + c, _N_CHUNK))
        gate_ref[:, 1024 + c:1024 + c + _N_CHUNK] = _silu(proj(_O_GD + c, _N_CHUNK))
    _store_vt(vt_ref, _dot_nt(wvt_ref[...], h), D_KV_HEADS)
    for c in range(0, 1024, _N_CHUNK):
        qkv_ref[:, c:c + _N_CHUNK] = (proj(_O_QKV + c, _N_CHUNK) * scale).astype(BF16)
    for c in range(1024, 3072, _N_CHUNK):
        qkv_ref[:, c:c + _N_CHUNK] = proj(_O_QKV + c, _N_CHUNK).astype(BF16)


def _odd_inproj(x2, pre_gain, w_perm, wvt, q_norm, k_norm, cos_ax, sin_ax):
    m = x2.shape[0]
    tm = TM_PROJ
    pos_blocks = SEQ // tm
    row = lambda i: (i, 0)
    fixed = lambda i: (0, 0)
    pos = lambda i: (i % pos_blocks, 0)
    return pl.pallas_call(
        _odd_inproj_kernel,
        grid=(m // tm,),
        in_specs=[
            pl.BlockSpec((tm, D_MODEL), row),
            pl.BlockSpec((1, D_MODEL), fixed),
            pl.BlockSpec((D_MODEL, _O_COLS), fixed, pipeline_mode=pl.Buffered(1)),
            pl.BlockSpec((256, D_MODEL), fixed, pipeline_mode=pl.Buffered(1)),
            pl.BlockSpec((1, HEAD_W), fixed),
            pl.BlockSpec((1, HEAD_W), fixed),
            pl.BlockSpec((tm, HEAD_W), pos),
            pl.BlockSpec((tm, HEAD_W), pos),
        ],
        out_specs=[
            pl.BlockSpec((tm, 3072), row),
            _vt_out_spec(D_KV_HEADS, tm),
            pl.BlockSpec((tm, 2048), row),
            pl.BlockSpec((tm, 1024), row),
            pl.BlockSpec((tm, 256), row),
        ],
        out_shape=[
            jax.ShapeDtypeStruct((m, 3072), BF16),
            _vt_shape(m, D_KV_HEADS),
            jax.ShapeDtypeStruct((m, 2048), F32),
            jax.ShapeDtypeStruct((m, 1024), BF16),
            jax.ShapeDtypeStruct((m, 256), BF16),
        ],
        compiler_params=pltpu.CompilerParams(
            dimension_semantics=("arbitrary",), vmem_limit_bytes=VMEM_LIMIT),
        name="odd_inproj",
    )(x2, pre_gain, w_perm, wvt, q_norm, k_norm, cos_ax, sin_ax)


_NA_Q = NA_GROUP_ROWS * GRID_W
_NA_K = NA_BAND_ROWS * GRID_W
_NA_GROUPS = SEQ // _NA_Q
_NA_ROWS = SEQ // GRID_W
_NA_LAST_START = _NA_ROWS - NA_BAND_ROWS
_NA_CLASS_GROUPS = (0, 1, _NA_GROUPS - 1)


def _na_band_start(g):
    lo = g * NA_GROUP_ROWS - NA_KH // 2
    if isinstance(g, int):
        return min(max(lo, 0), _NA_LAST_START)
    return jnp.clip(lo, 0, _NA_LAST_START)


def _na_build_bias(rpb_ref, bias_ref):
    vec = rpb_ref[...] * LOG2E
    even_src = pltpu.roll(vec, 128 - (NA_KW - 1), 1)
    odd_src = pltpu.roll(vec, GRID_W - (NA_KW - 1), 1)
    q_idx = lax.broadcasted_iota(jnp.int32, (GRID_W, 128), 0)
    lane = lax.broadcasted_iota(jnp.int32, (GRID_W, 128), 1)
    w_idx = lane & (GRID_W - 1)
    c0 = jnp.clip(q_idx - NA_KW // 2, 0, GRID_W - NA_KW)
    col_in = (w_idx >= c0) & (w_idx < c0 + NA_KW)
    neg = jnp.full((GRID_W, 128), NEG_INF, F32)
    cache = {}
    for cls, g in enumerate(_NA_CLASS_GROUPS):
        start = _na_band_start(g)
        for qr in range(NA_GROUP_ROWS):
            rq = g * NA_GROUP_ROWS + qr
            r0 = min(max(rq - NA_KH // 2, 0), _NA_ROWS - NA_KH)
            for kp in range(NA_BAND_ROWS // 2):
                halves = []
                for rk in (start + 2 * kp, start + 2 * kp + 1):
                    halves.append(rk - rq + NA_KH - 1 if r0 <= rk < r0 + NA_KH else None)
                key = tuple(halves)
                if key not in cache:
                    if halves[0] is None and halves[1] is None:
                        cache[key] = neg
                    else:
                        src = jnp.zeros((1, 128), F32)
                        mask = None
                        if halves[0] is not None:
                            src = src + even_src[halves[0]:halves[0] + 1, :]
                            mask = lane < GRID_W
                        if halves[1] is not None:
                            src = src + odd_src[halves[1]:halves[1] + 1, :]
                            mask = (lane >= GRID_W) if mask is None else None
                        tile = pltpu.roll(jnp.broadcast_to(src, (GRID_W, 128)), 0, 1, stride=1, stride_axis=0)
                        keep = col_in if mask is None else (col_in & mask)
                        cache[key] = jnp.where(keep, tile, neg)
                bias_ref[cls, qr * GRID_W:(qr + 1) * GRID_W, kp * 128:(kp + 1) * 128] = cache[key]


def _na_kernel(q_ref, k_ref, v_ref, rpb_ref, g_ref, o_ref, bias_ref):
    g = pl.program_id(2)

    @pl.when(g == 0)
    def _():
        for j in range(NA_HEADS_PER_STEP):
            _na_build_bias(rpb_ref.at[j], bias_ref.at[j])

    cls = jnp.where(g == 0, 0, jnp.where(g == _NA_GROUPS - 1, 2, 1))
    start = pl.multiple_of(_na_band_start(g) * GRID_W, GRID_W)
    own = pl.multiple_of(g * _NA_Q, _NA_Q)
    heads = [slice(j * HEAD_W, (j + 1) * HEAD_W) for j in range(NA_HEADS_PER_STEP)]
    ones = jnp.ones((_NA_K, HEAD_W), BF16)

    def scores(j):
        return _dot_nt(q_ref[:, heads[j]], k_ref[pl.ds(start, _NA_K), heads[j]]) + bias_ref[j, cls]

    def attend(j, s, ref):
        p = jnp.exp2(s - ref).astype(BF16)
        return _dot(p, jnp.concatenate([v_ref[pl.ds(start, _NA_K), heads[j]], ones], axis=1))

    def emit(j, ol):
        o = ol[:, :HEAD_W] * (1.0 / ol[:, HEAD_W:HEAD_W + 1])
        o_ref[:, heads[j]] = (o * g_ref[:, heads[j]]).astype(BF16)

    worst = None
    for j in range(NA_HEADS_PER_STEP):
        q_own = q_ref[:, heads[j]].astype(F32) * k_ref[pl.ds(own, _NA_Q), heads[j]].astype(F32)
        self_score = jnp.sum(q_own, axis=1, keepdims=True) + rpb_ref[j, NA_KH - 1:NA_KH, NA_KW - 1:NA_KW] * LOG2E
        ol = attend(j, scores(j), self_score)
        emit(j, ol)
        finite = jnp.min(jnp.where(jnp.isfinite(ol), 1.0, 0.0))
        worst = finite if worst is None else jnp.minimum(worst, finite)

    @pl.when(worst < 0.5)
    def _():
        for j in range(NA_HEADS_PER_STEP):
            s = scores(j)
            emit(j, attend(j, s, jnp.max(s, axis=1, keepdims=True)))


def _na_attn(qkv, gate, rpb_pad):
    b, s, _ = qkv.shape
    hs = NA_HEADS_PER_STEP
    steps = HEADS // hs
    return pl.pallas_call(
        _na_kernel,
        grid=(b, steps, _NA_GROUPS),
        in_specs=[
            pl.BlockSpec((None, _NA_Q, hs * HEAD_W), lambda bb, h, g: (bb, g, h)),
            pl.BlockSpec((None, s, hs * HEAD_W), lambda bb, h, g: (bb, 0, steps + h)),
            pl.BlockSpec((None, s, hs * HEAD_W), lambda bb, h, g: (bb, 0, 2 * steps + h)),
            pl.BlockSpec((hs, 16, 128), lambda bb, h, g: (h, 0, 0)),
            pl.BlockSpec((None, _NA_Q, hs * HEAD_W), lambda bb, h, g: (bb, g, h)),
        ],
        out_specs=pl.BlockSpec((None, _NA_Q, hs * HEAD_W), lambda bb, h, g: (bb, g, h)),
        out_shape=jax.ShapeDtypeStruct((b, s, HEADS * HEAD_W), BF16),
        scratch_shapes=[pltpu.VMEM((hs, 3, _NA_Q, _NA_K), F32)],
        compiler_params=pltpu.CompilerParams(
            dimension_semantics=("arbitrary",) * 3, vmem_limit_bytes=VMEM_LIMIT),
        name="na_attn",
    )(qkv, qkv, qkv, rpb_pad, gate)


def _rope_seg_tables(pos):
    inv = ROPE_THETA ** (-np.arange(ROPE_HALF, dtype=np.float64) / ROPE_HALF)
    ang = pos.astype(np.float64)[:, None] * inv[None, :]
    cos, sin = np.cos(ang), np.sin(ang)
    return np.concatenate([cos, cos], axis=1), np.concatenate([-sin, sin], axis=1)


def _rope_tables():
    t = np.arange(SEQ)
    cos_t, sin_t = _rope_seg_tables(t)
    ones, zeros = np.ones((SEQ, ROPE_SEG)), np.zeros((SEQ, ROPE_SEG))
    cos_k = np.concatenate([cos_t, ones], axis=1)
    sin_k = np.concatenate([sin_t, zeros], axis=1)
    cos_q = np.concatenate([ones, ones, cos_t, ones], axis=1)
    sin_q = np.concatenate([zeros, zeros, sin_t, zeros], axis=1)
    cos_r, sin_r = _rope_seg_tables(t // GRID_W)
    cos_c, sin_c = _rope_seg_tables(t % GRID_W)
    cos_ax = np.concatenate([cos_r, cos_c], axis=1)
    sin_ax = np.concatenate([sin_r, sin_c], axis=1)
    as_f32 = lambda a: jnp.asarray(a, F32)
    return tuple(map(as_f32, (cos_k, sin_k, cos_q, sin_q, cos_ax, sin_ax)))


def _transpose_cols_kernel(w_ref, o_ref):
    o_ref[...] = w_ref[...].T.astype(BF16)


def _transpose_cols(w, col0, width):
    k = w.shape[0]
    rows = 256
    assert col0 % width == 0 and k % rows == 0
    return pl.pallas_call(
        _transpose_cols_kernel,
        grid=(k // rows,),
        in_specs=[pl.BlockSpec((rows, width), lambda i: (i, col0 // width))],
        out_specs=pl.BlockSpec((width, rows), lambda i: (0, i)),
        out_shape=jax.ShapeDtypeStruct((width, k), BF16),
        name="transpose_cols",
    )(w)


def _transposed_cast(wt):
    n, k = wt.shape
    rows = 256
    return pl.pallas_call(
        _transpose_cols_kernel,
        grid=(pl.cdiv(n, rows),),
        in_specs=[pl.BlockSpec((rows, k), lambda i: (i, 0))],
        out_specs=pl.BlockSpec((k, rows), lambda i: (0, i)),
        out_shape=jax.ShapeDtypeStruct((k, n), BF16),
        name="transposed_cast",
    )(wt)


def _even_weights(w_in, w_uq, w_ukv):
    w_t = jnp.swapaxes(w_in, 0, 1)
    w_bf = _transposed_cast(w_t)
    va_t = w_t[2048:3072].astype(BF16)
    wq = w_uq.reshape(B_Q_RANK, HEADS, B_NOPE + B_ROPE)
    wq = jnp.pad(wq, ((0, 0), (0, 0), (0, B_QK_PAD - B_NOPE - B_ROPE))).reshape(B_Q_RANK, HEADS * B_QK_PAD)
    wkv = w_ukv.reshape(B_KV_RANK, HEADS, B_NOPE + HEAD_W)
    wk = wkv[:, :, :B_NOPE].reshape(B_KV_RANK, HEADS * B_NOPE)
    wv = wkv[:, :, B_NOPE:].reshape(B_KV_RANK, HEADS * HEAD_W)
    return (w_bf, va_t, wq.astype(BF16), wk.astype(BF16), wv.T.astype(BF16))


def _odd_weights(w_in):
    return w_in.astype(BF16), _transpose_cols(w_in, 5376, 256)


def _even_layer(x2, batch, layer, pre_gain, post_gain, w_in, w_out, lq1, lk1, lq2, lk2, subln,
                q_norm, w_uq, kv_norm, w_ukv, tables):
    cos_k, sin_k, cos_q, sin_q, _, _ = tables
    lambda_init = 0.8 - 0.6 * math.exp(-0.3 * layer)
    w_perm, wvat, wq, wk, wvbt = _even_weights(w_in, w_uq, w_ukv)
    qk, vat, gate, cq, ckv, kpe = _even_inproj(
        x2, pre_gain[None], w_perm, wvat, q_norm[None], kv_norm[None], cos_k, sin_k)
    as3 = lambda a: a.reshape(batch, SEQ, a.shape[-1])
    slopes = jnp.asarray(2.0 ** (-8.0 * np.arange(1, HEADS + 1) / HEADS), F32)
    qaug, kaug = _alibi_aug_tables()
    oa = _diff_attn(slopes, as3(qk), vat, as3(gate), qaug, kaug, lq1[None], lk1[None], lq2[None],
                    lk2[None], subln[None], lambda_init)
    q_b, k_b, vbt = _mla_up(cq, ckv, kpe, wq, wk, wvbt, cos_q, sin_q)
    ob = _gated_attn(as3(q_b), as3(k_b), vbt, as3(gate), d_qk=B_QK_PAD, q_heads_per_kv=1,
                     heads_per_step=2, gate_block0=HEADS, name="latent_attn")
    m = x2.shape[0]
    return _outproj(oa.reshape(m, -1), ob.reshape(m, -1), w_out.astype(BF16), x2, post_gain[None],
                    "even_outproj")


def _odd_layer(x2, batch, pre_gain, post_gain, w_in, w_out, rpb, q_norm, k_norm, tables):
    cos_ax, sin_ax = tables[4], tables[5]
    w_perm, wvdt = _odd_weights(w_in)
    qkv, vdt, gate, qd, kd = _odd_inproj(
        x2, pre_gain[None], w_perm, wvdt, q_norm[None], k_norm[None], cos_ax, sin_ax)
    as3 = lambda a: a.reshape(batch, SEQ, a.shape[-1])
    rpb_pad = jnp.pad(rpb, ((0, 0), (0, 16 - rpb.shape[1]), (0, 128 - rpb.shape[2])))
    oc = _na_attn(as3(qkv), as3(gate), rpb_pad)
    od = _gated_attn(as3(qd), as3(kd), vdt, as3(gate), d_qk=HEAD_W,
                     q_heads_per_kv=HEADS // D_KV_HEADS, heads_per_step=2, gate_block0=HEADS,
                     name="gqa_attn")
    m = x2.shape[0]
    return _outproj(oc.reshape(m, -1), od.reshape(m, -1), w_out.astype(BF16), x2, post_gain[None],
                    "odd_outproj")


def kernel(x, pre_norm, post_norm, even_w_in, even_w_out, diff_lambda_q1, diff_lambda_k1,
           diff_lambda_q2, diff_lambda_k2, diff_subln, mla_q_norm, mla_w_uq, mla_kv_norm, mla_w_ukv,
           odd_w_in, odd_w_out, na_rpb, gqa_q_norm, gqa_k_norm):
    batch, seq, d_model = x.shape
    assert (seq, d_model) == (SEQ, D_MODEL)
    depth = pre_norm.shape[0]
    tables = _rope_tables()
    x2 = x.reshape(batch * seq, d_model)
    for layer in range(depth):
        i = layer // 2
        if layer % 2 == 0:
            x2 = _even_layer(x2, batch, layer, pre_norm[layer], post_norm[layer], even_w_in[i],
                             even_w_out[i], diff_lambda_q1[i], diff_lambda_k1[i], diff_lambda_q2[i],
                             diff_lambda_k2[i], diff_subln[i], mla_q_norm[i], mla_w_uq[i],
                             mla_kv_norm[i], mla_w_ukv[i], tables)
        else:
            x2 = _odd_layer(x2, batch, pre_norm[layer], post_norm[layer], odd_w_in[i], odd_w_out[i],
                            na_rpb[i], gqa_q_norm[i], gqa_k_norm[i], tables)
    return x2.reshape(batch, seq, d_model)
```
